```python
import jax
import jax.numpy as jnp
from jax import lax
import numpy as np

D_MODEL = 4096
BATCH = 4
SEQ = 4096
DEPTH = 1

CTX_LEN = 256
GRID_W = 64
HEAD_DIM = 128
N_MLP_GROUPS = (D_MODEL // 2) // HEAD_DIM
MLP_WIDTH = N_MLP_GROUPS * HEAD_DIM
N_Q_HEADS = (D_MODEL // 2) // HEAD_DIM
N_KV_HEADS = N_Q_HEADS // 4
Q_PER_KV = N_Q_HEADS // N_KV_HEADS
ATTN_WIDTH = N_Q_HEADS * HEAD_DIM
KV_WIDTH = N_KV_HEADS * HEAD_DIM
MIX_WIDTH = MLP_WIDTH + ATTN_WIDTH
IN_WIDTH = 2 * MLP_WIDTH + ATTN_WIDTH + 2 * KV_WIDTH
SPLITS = (MLP_WIDTH, 2 * MLP_WIDTH, 2 * MLP_WIDTH + ATTN_WIDTH, 2 * MLP_WIDTH + ATTN_WIDTH + KV_WIDTH)
CHUNK = 128
WINDOW = 128
BLOCK = 128
D_FF = ((8 * D_MODEL // 3 + 255) // 256) * 256
N_SUB = 3
N_MOD = 3
ROPE_BASE = 10000.0
EPS = 1e-6
ALPHA = (2.0 * DEPTH) ** 0.25
BETA = (8.0 * DEPTH) ** -0.25

kernel_name = 'hymba_gmlp_swa_deepnorm_dit_block'


def standardize(x):
    xf = x.astype(jnp.float32)
    mu = jnp.mean(xf, -1, keepdims=True)
    var = jnp.mean(jnp.square(xf - mu), -1, keepdims=True)
    return (xf - mu) * lax.rsqrt(var + EPS)


def layer_norm(x, gain, bias):
    y = standardize(x) * gain.astype(jnp.float32) + bias.astype(jnp.float32)
    return y.astype(x.dtype)


def rms_normalize(y):
    yf = y.astype(jnp.float32)
    return yf * lax.rsqrt(jnp.mean(jnp.square(yf), -1, keepdims=True) + EPS)


def modulate(h, shift, scale):
    return h * (1.0 + scale) + shift


def mod_terms(m, s):
    return m[:, s, 0][:, None, :], m[:, s, 1][:, None, :], m[:, s, 2][:, None, :]


def swiglu(h, w_gate, w_up, w_down):
    return (jax.nn.silu(h @ w_gate) * (h @ w_up)) @ w_down


def ffn_sublayer(h, m, s, w_gate, w_up, w_down, gain, bias):
    shift, scale, gate = mod_terms(m, s)
    y = swiglu(modulate(h, shift, scale), w_gate, w_up, w_down)
    return layer_norm(ALPHA * h + 0.5 * gate * y, gain, bias)


def split_heads(t, n_heads):
    return t.reshape(t.shape[:-1] + (n_heads, HEAD_DIM))


def project(h, w):
    u, v, q, k, va = jnp.split(h @ w, SPLITS, axis=-1)
    return (jax.nn.gelu(u, approximate=False), jax.nn.gelu(v, approximate=False),
            split_heads(q, N_Q_HEADS), split_heads(k, N_KV_HEADS), split_heads(va, N_KV_HEADS))


def axial_rope_tables(n):
    rows = n // GRID_W
    row = jnp.broadcast_to(jnp.arange(rows, dtype=jnp.float32)[:, None], (rows, GRID_W)).reshape(n)
    col = jnp.broadcast_to(jnp.arange(GRID_W, dtype=jnp.float32)[None, :], (rows, GRID_W)).reshape(n)
    n_freq = HEAD_DIM // 4
    inv_freq = ROPE_BASE ** (-jnp.arange(n_freq, dtype=jnp.float32) / n_freq)
    ang = jnp.concatenate([row[:, None] * inv_freq, col[:, None] * inv_freq], axis=-1)
    return jnp.cos(ang), jnp.sin(ang)


def apply_rope(t, cos, sin):
    tf = t.astype(jnp.float32)
    t1, t2 = jnp.split(tf, 2, axis=-1)
    cs, sn = cos[None, :, None, :], sin[None, :, None, :]
    return jnp.concatenate([t1 * cs - t2 * sn, t2 * cs + t1 * sn], axis=-1).astype(t.dtype)


def chunk_spatial_gating(u, v, w_s, b_s):
    b, n, _ = u.shape
    vn = standardize(v).astype(v.dtype).reshape(b, n // CHUNK, CHUNK, N_MLP_GROUPS, HEAD_DIM)
    mixed = jnp.einsum('gpq,bcqgd->bcpgd', w_s, vn) + b_s.T[:, :, None]
    return u * mixed.reshape(u.shape)


def sink_attention(q, k, v, sink, mask):
    s = jnp.einsum('bqhgd,bkhd->bhgqk', q, k).astype(jnp.float32) * (HEAD_DIM ** -0.5)
    if mask is not None:
        s = jnp.where(mask, s, -jnp.inf)
    sink_col = jnp.broadcast_to(sink.astype(jnp.float32)[None, :, :, None, None], s.shape[:-1] + (1,))
    p = jax.nn.softmax(jnp.concatenate([sink_col, s], axis=-1), axis=-1)[..., 1:]
    return jnp.einsum('bhgqk,bkhd->bqhgd', p.astype(v.dtype), v)


def windowed_context_attention(q, k, v, kc, vc, sink):
    b, n = q.shape[:2]
    nb = n // BLOCK
    c_len = kc.shape[1]
    qb = jnp.moveaxis(q.reshape(b, nb, BLOCK, N_KV_HEADS, Q_PER_KV, HEAD_DIM), 1, 0)
    pad = ((0, 0), (BLOCK, BLOCK), (0, 0), (0, 0))
    kp, vp = jnp.pad(k, pad), jnp.pad(v, pad)
    q_off = jnp.arange(BLOCK)[:, None]
    k_off = jnp.arange(3 * BLOCK)[None, :] - BLOCK
    in_band = jnp.abs(k_off - q_off) <= WINDOW
    ctx_mask = jnp.ones((BLOCK, c_len), dtype=bool)

    def one_block(args):
        qi, i = args
        kw = lax.dynamic_slice_in_dim(kp, i * BLOCK, 3 * BLOCK, axis=1)
        vw = lax.dynamic_slice_in_dim(vp, i * BLOCK, 3 * BLOCK, axis=1)
        kpos = i * BLOCK + k_off
        mask = jnp.concatenate([ctx_mask, in_band & (kpos >= 0) & (kpos < n)], axis=1)
        return sink_attention(qi, jnp.concatenate([kc, kw], axis=1), jnp.concatenate([vc, vw], axis=1), sink, mask)

    out = lax.map(one_block, (qb, jnp.arange(nb)))
    return jnp.moveaxis(out, 0, 1).reshape(b, n, ATTN_WIDTH)


def merge_mixers(y_mlp, y_attn, g_mix, w_out):
    y = jnp.concatenate([rms_normalize(y_mlp), rms_normalize(y_attn)], axis=-1) * g_mix.astype(jnp.float32)
    return y.astype(w_out.dtype) @ w_out


def setup_inputs(seed: int = 0) -> dict:
    key = jax.random.key(seed)
    ks = jax.random.split(key, 17)

    def nrm(k, shape, scale):
        return jax.random.normal(k, shape, jnp.float32) * scale

    return {
        'x': nrm(ks[0], (BATCH, SEQ, D_MODEL), 1.0),
        'c': nrm(ks[1], (BATCH, D_MODEL), 1.0),
        'ctx': nrm(ks[2], (BATCH, CTX_LEN, D_MODEL), 1.0),
        'c_ctx': nrm(ks[3], (D_MODEL,), 1.0),
        'w_ada': nrm(ks[4], (DEPTH, D_MODEL, N_SUB * N_MOD * D_MODEL), 0.5 * D_MODEL ** -0.5),
        'b_ada': nrm(ks[5], (DEPTH, N_SUB * N_MOD * D_MODEL), 0.02),
        'w_ffn_gate': nrm(ks[6], (DEPTH, 2, D_MODEL, D_FF), D_MODEL ** -0.5),
        'w_ffn_up': nrm(ks[7], (DEPTH, 2, D_MODEL, D_FF), D_MODEL ** -0.5),
        'w_ffn_down': nrm(ks[8], (DEPTH, 2, D_FF, D_MODEL), BETA * D_FF ** -0.5),
        'w_in': nrm(ks[9], (DEPTH, D_MODEL, IN_WIDTH), D_MODEL ** -0.5),
        'w_spatial': nrm(ks[10], (DEPTH, N_MLP_GROUPS, CHUNK, CHUNK), CHUNK ** -0.5),
        'b_spatial': 1.0 + nrm(ks[11], (DEPTH, N_MLP_GROUPS, CHUNK), 0.02),
        'sink_logit': nrm(ks[12], (DEPTH, N_Q_HEADS), 0.5),
        'g_mix': 1.0 + nrm(ks[13], (DEPTH, MIX_WIDTH), 0.02),
        'w_out': nrm(ks[14], (DEPTH, MIX_WIDTH, D_MODEL), BETA * MIX_WIDTH ** -0.5),
        'ln_gain': 1.0 + nrm(ks[15], (DEPTH, N_SUB, D_MODEL), 0.02),
        'ln_bias': nrm(ks[16], (DEPTH, N_SUB, D_MODEL), 0.02),
    }


def reference(x, c, ctx, c_ctx, w_ada, b_ada, w_ffn_gate, w_ffn_up, w_ffn_down, w_in, w_spatial,
              b_spatial, sink_logit, g_mix, w_out, ln_gain, ln_bias):
    b, n, d = x.shape
    c_len = ctx.shape[1]
    cos, sin = axial_rope_tables(n)
    for layer in range(DEPTH):
        update_ctx = layer < DEPTH - 1
        m_x = (jax.nn.silu(c) @ w_ada[layer] + b_ada[layer]).reshape(b, N_SUB, N_MOD, d)
        m_c = (jax.nn.silu(c_ctx) @ w_ada[layer] + b_ada[layer]).reshape(1, N_SUB, N_MOD, d)
        sink = sink_logit[layer].reshape(N_KV_HEADS, Q_PER_KV)
        ffn_a = (w_ffn_gate[layer, 0], w_ffn_up[layer, 0], w_ffn_down[layer, 0], ln_gain[layer, 0], ln_bias[layer, 0])
        ffn_b = (w_ffn_gate[layer, 1], w_ffn_up[layer, 1], w_ffn_down[layer, 1], ln_gain[layer, 2], ln_bias[layer, 2])

        x = ffn_sublayer(x, m_x, 0, *ffn_a)
        ctx = ffn_sublayer(ctx, m_c, 0, *ffn_a)

        sx, scx, gx = mod_terms(m_x, 1)
        sc, scc, gc = mod_terms(m_c, 1)
        hx = modulate(x, sx, scx)
        hc = modulate(ctx, sc, scc)
        ux, vgx, qx, kx, vx = project(hx, w_in[layer])
        qx = apply_rope(qx, cos, sin)
        kx = apply_rope(kx, cos, sin)
        if update_ctx:
            uc, vgc, qc, kc, vc = project(hc, w_in[layer])
        else:
            kc, vc = jnp.split(hc @ w_in[layer, :, SPLITS[2]:], [KV_WIDTH], axis=-1)
            kc, vc = split_heads(kc, N_KV_HEADS), split_heads(vc, N_KV_HEADS)
        y_attn = windowed_context_attention(qx, kx, vx, kc, vc, sink)
        y_mlp = chunk_spatial_gating(ux, vgx, w_spatial[layer], b_spatial[layer])
        mix = merge_mixers(y_mlp, y_attn, g_mix[layer], w_out[layer])
        x_mid = layer_norm(ALPHA * x + gx * mix, ln_gain[layer, 1], ln_bias[layer, 1])

        if update_ctx:
            qc5 = qc.reshape(b, c_len, N_KV_HEADS, Q_PER_KV, HEAD_DIM)
            yc_attn = sink_attention(qc5, kc, vc, sink, None).reshape(b, c_len, ATTN_WIDTH)
            yc_mlp = chunk_spatial_gating(uc, vgc, w_spatial[layer], b_spatial[layer])
            mix_c = merge_mixers(yc_mlp, yc_attn, g_mix[layer], w_out[layer])
            ctx = layer_norm(ALPHA * ctx + gc * mix_c, ln_gain[layer, 1], ln_bias[layer, 1])
            ctx = ffn_sublayer(ctx, m_c, 2, *ffn_b)

        x = ffn_sublayer(x_mid, m_x, 2, *ffn_b)
    return x
```

```python
import functools

import jax
import jax.numpy as jnp
import numpy as np
from jax import lax
from jax.experimental import pallas as pl
from jax.experimental.pallas import tpu as pltpu

F32 = jnp.float32
BF16 = jnp.bfloat16

HEAD_DIM = 128
CHUNK = 128
WINDOW = 128
BLOCK = 128
GRID_W = 64
ROPE_BASE = 10000.0
EPS = 1e-6
N_SUB = 3
N_MOD = 3

VMEM_LIMIT_BYTES = 58 * 1024 * 1024
FF_PAD_MULTIPLE = 1024


def _cparams(sem):
    return pltpu.CompilerParams(dimension_semantics=sem, vmem_limit_bytes=VMEM_LIMIT_BYTES)


def _ada_kernel(c_ref, w_ref, b_ref, o_ref):
    c = c_ref[...]
    s = (c * jax.nn.sigmoid(c)).astype(BF16)
    o_ref[...] = jnp.dot(s, w_ref[...].astype(BF16), preferred_element_type=F32) + b_ref[...]


def _ada(c8, w_ada, b_ada, tn=512):
    d, n = w_ada.shape
    return pl.pallas_call(
        _ada_kernel,
        grid=(n // tn,),
        in_specs=[pl.BlockSpec((8, d), lambda j: (0, 0)),
                  pl.BlockSpec((d, tn), lambda j: (0, j)),
                  pl.BlockSpec((1, tn), lambda j: (0, j))],
        out_specs=pl.BlockSpec((8, tn), lambda j: (0, j)),
        out_shape=jax.ShapeDtypeStruct((8, n), F32),
        compiler_params=_cparams(("arbitrary",)),
        name="ada",
    )(c8, w_ada, b_ada.reshape(1, n))


def _modcast_kernel(x_ref, shift_ref, scale_ref, o_ref):
    o_ref[...] = (x_ref[...] * (1.0 + scale_ref[...]) + shift_ref[...]).astype(BF16)


def _modcast(x2d, shift, scale, seg_fn, tm=256):
    r, d = x2d.shape
    vec = pl.BlockSpec((None, 1, d), lambda i: (seg_fn(i * tm), 0, 0))
    return pl.pallas_call(
        _modcast_kernel,
        grid=(r // tm,),
        in_specs=[pl.BlockSpec((tm, d), lambda i: (i, 0)), vec, vec],
        out_specs=pl.BlockSpec((tm, d), lambda i: (i, 0)),
        out_shape=jax.ShapeDtypeStruct((r, d), BF16),
        compiler_params=_cparams(("arbitrary",)),
        name="modcast",
    )(x2d, shift, scale)


def _swiglu_up_kernel(x_ref, wg_ref, wu_ref, o_ref):
    x = x_ref[...]
    g = jnp.dot(x, wg_ref[...], preferred_element_type=F32)
    u = jnp.dot(x, wu_ref[...], preferred_element_type=F32)
    o_ref[...] = (g * jax.nn.sigmoid(g) * u).astype(BF16)


def _swiglu_up(xm, wg, wu, tm=1024, tn=512):
    r, d = xm.shape
    f = wg.shape[1]
    return pl.pallas_call(
        _swiglu_up_kernel,
        grid=(r // tm, f // tn),
        in_specs=[pl.BlockSpec((tm, d), lambda i, j: (i, 0)),
                  pl.BlockSpec((d, tn), lambda i, j: (0, j)),
                  pl.BlockSpec((d, tn), lambda i, j: (0, j))],
        out_specs=pl.BlockSpec((tm, tn), lambda i, j: (i, j)),
        out_shape=jax.ShapeDtypeStruct((r, f), BF16),
        compiler_params=_cparams(("arbitrary", "arbitrary")),
        name="swiglu_up",
    )(xm, wg, wu)


def _mm_kernel(a_ref, b_ref, o_ref, *, nk):
    d = jnp.dot(a_ref[...], b_ref[...], preferred_element_type=F32)
    if nk == 1:
        o_ref[...] = d.astype(o_ref.dtype)
    else:
        k = pl.program_id(2)

        @pl.when(k == 0)
        def _():
            o_ref[...] = d

        @pl.when(k > 0)
        def _():
            o_ref[...] += d


def _matmul(a, b, out_dtype, tm, tn, tk, name):
    r, kk = a.shape
    n = b.shape[1]
    nk = kk // tk
    assert nk == 1 or out_dtype == F32
    return pl.pallas_call(
        functools.partial(_mm_kernel, nk=nk),
        grid=(r // tm, n // tn, nk),
        in_specs=[pl.BlockSpec((tm, tk), lambda i, j, k: (i, k)),
                  pl.BlockSpec((tk, tn), lambda i, j, k: (k, j))],
        out_specs=pl.BlockSpec((tm, tn), lambda i, j, k: (i, j)),
        out_shape=jax.ShapeDtypeStruct((r, n), out_dtype),
        compiler_params=_cparams(("arbitrary", "arbitrary", "arbitrary")),
        name=name,
    )(a, b)


def _resid_ln_kernel(*refs, alpha, coef, emit_mod):
    if emit_mod:
        h_ref, y_ref, gate_ref, gain_ref, bias_ref, shift_ref, scale_ref, o_ref, om_ref = refs
    else:
        h_ref, y_ref, gate_ref, gain_ref, bias_ref, o_ref = refs
    z = alpha * h_ref[...] + (coef * gate_ref[...]) * y_ref[...]
    mu = jnp.mean(z, axis=-1, keepdims=True)
    zc = z - mu
    var = jnp.mean(zc * zc, axis=-1, keepdims=True)
    hn = zc * lax.rsqrt(var + EPS) * gain_ref[...] + bias_ref[...]
    o_ref[...] = hn
    if emit_mod:
        om_ref[...] = (hn * (1.0 + scale_ref[...]) + shift_ref[...]).astype(BF16)


def _resid_ln(h, y, gate, gain, bias, seg_fn, alpha, coef, next_mod=None, tm=256):
    r, d = h.shape
    row = pl.BlockSpec((tm, d), lambda i: (i, 0))
    vec = pl.BlockSpec((None, 1, d), lambda i: (seg_fn(i * tm), 0, 0))
    full = pl.BlockSpec((1, d), lambda i: (0, 0))
    emit_mod = next_mod is not None
    in_specs = [row, row, vec, full, full]
    args = [h, y, gate, gain.reshape(1, d), bias.reshape(1, d)]
    out_specs = [row]
    out_shape = [jax.ShapeDtypeStruct((r, d), F32)]
    if emit_mod:
        in_specs += [vec, vec]
        args += list(next_mod)
        out_specs.append(row)
        out_shape.append(jax.ShapeDtypeStruct((r, d), BF16))
    outs = pl.pallas_call(
        functools.partial(_resid_ln_kernel, alpha=alpha, coef=coef, emit_mod=emit_mod),
        grid=(r // tm,),
        in_specs=in_specs,
        out_specs=out_specs,
        out_shape=out_shape,
        compiler_params=_cparams(("arbitrary",)),
        name="resid_ln",
    )(*args)
    return outs if emit_mod else (outs[0], None)


def _proj_kernel(x_ref, w_ref, cos_ref, sin_ref, o_ref, *, n_gelu, n_rope, heads_per_tile):
    j = pl.program_id(1)
    acc = jnp.dot(x_ref[...], w_ref[...], preferred_element_type=F32)

    @pl.when(j < n_gelu)
    def _():
        o_ref[...] = (0.5 * acc * (1.0 + lax.erf(acc * np.float32(np.sqrt(0.5))))).astype(BF16)

    @pl.when((j >= n_gelu) & (j < n_gelu + n_rope))
    def _():
        cs = cos_ref[...]
        sn = sin_ref[...]
        for hh in range(heads_per_tile):
            t = acc[:, hh * HEAD_DIM:(hh + 1) * HEAD_DIM]
            rot = pltpu.roll(t, HEAD_DIM // 2, 1)
            o_ref[:, hh * HEAD_DIM:(hh + 1) * HEAD_DIM] = (t * cs + rot * sn).astype(BF16)

    @pl.when(j >= n_gelu + n_rope)
    def _():
        o_ref[...] = acc.astype(BF16)


def _proj(hx, w_in, cosf, sinf, seq, n_gelu_cols, n_rope_cols, tm=1024, tn=512):
    r, d = hx.shape
    n = w_in.shape[1]
    rows_per_seq = seq // tm
    return pl.pallas_call(
        functools.partial(_proj_kernel, n_gelu=n_gelu_cols // tn, n_rope=n_rope_cols // tn,
                          heads_per_tile=tn // HEAD_DIM),
        grid=(r // tm, n // tn),
        in_specs=[pl.BlockSpec((tm, d), lambda i, j: (i, 0)),
                  pl.BlockSpec((d, tn), lambda i, j: (0, j)),
                  pl.BlockSpec((tm, HEAD_DIM), lambda i, j: (i % rows_per_seq, 0)),
                  pl.BlockSpec((tm, HEAD_DIM), lambda i, j: (i % rows_per_seq, 0))],
        out_specs=pl.BlockSpec((tm, tn), lambda i, j: (i, j)),
        out_shape=jax.ShapeDtypeStruct((r, n), BF16),
        compiler_params=_cparams(("arbitrary", "arbitrary")),
        name="in_proj",
    )(hx, w_in, cosf, sinf)


def _mixer_kernel(sink_ref, gu_ref, gv_ref, q_ref, kp_ref, kc_ref, kn_ref, vp_ref, vc_ref, vn_ref,
                  kctx_ref, vctx_ref, ws_ref, bs_ref, gmix_ref, o_ref, ybuf,
                  *, n_groups, n_kv, q_per_kv, c_len, seq):
    blk = pl.program_id(1)
    mlp_w = n_groups * HEAD_DIM
    attn_w = n_kv * q_per_kv * HEAD_DIM

    gv = gv_ref[...].astype(F32)
    mu = jnp.mean(gv, axis=-1, keepdims=True)
    vc_ = gv - mu
    var = jnp.mean(vc_ * vc_, axis=-1, keepdims=True)
    vstd = (vc_ * lax.rsqrt(var + EPS)).astype(BF16)
    ss_mlp = jnp.zeros((CHUNK, 1), F32)
    for g in range(n_groups):
        sl = slice(g * HEAD_DIM, (g + 1) * HEAD_DIM)
        mixed = jnp.dot(ws_ref[g], vstd[:, sl], preferred_element_type=F32) + bs_ref[:, g:g + 1]
        ym = gu_ref[:, sl].astype(F32) * mixed
        ss_mlp = ss_mlp + jnp.sum(ym * ym, axis=-1, keepdims=True)
        ybuf[:, sl] = ym

    n_keys = c_len + 3 * BLOCK
    rows = q_per_kv * BLOCK
    r_i = lax.broadcasted_iota(jnp.int32, (rows, n_keys), 0) & (BLOCK - 1)
    col = lax.broadcasted_iota(jnp.int32, (rows, n_keys), 1)
    k_off = col - (c_len + BLOCK)
    kpos = blk * BLOCK + k_off
    valid = (col < c_len) | ((jnp.abs(k_off - r_i) <= WINDOW) & (kpos >= 0) & (kpos < seq))
    scale = np.float32(HEAD_DIM ** -0.5)
    ss_attn = jnp.zeros((BLOCK, 1), F32)
    for h in range(n_kv):
        hs = slice(h * HEAD_DIM, (h + 1) * HEAD_DIM)
        q4 = jnp.concatenate(
            [q_ref[:, (h * q_per_kv + g) * HEAD_DIM:(h * q_per_kv + g + 1) * HEAD_DIM] for g in range(q_per_kv)],
            axis=0)
        keys = jnp.concatenate([kctx_ref[:, hs], kp_ref[:, hs], kc_ref[:, hs], kn_ref[:, hs]], axis=0)
        vals = jnp.concatenate([vctx_ref[:, hs], vp_ref[:, hs], vc_ref[:, hs], vn_ref[:, hs]], axis=0)
        s = lax.dot_general(q4, keys, (((1,), (1,)), ((), ())), preferred_element_type=F32) * scale
        s = jnp.where(valid, s, -jnp.inf)
        sink_col = jnp.concatenate(
            [jnp.full((BLOCK, 1), sink_ref[h * q_per_kv + g], F32) for g in range(q_per_kv)], axis=0)
        m = jnp.maximum(jnp.max(s, axis=-1, keepdims=True), sink_col)
        p = jnp.exp(s - m)
        denom = jnp.sum(p, axis=-1, keepdims=True) + jnp.exp(sink_col - m)
        o = jnp.dot(p.astype(BF16), vals, preferred_element_type=F32) * (1.0 / denom)
        for g in range(q_per_kv):
            og = o[g * BLOCK:(g + 1) * BLOCK, :]
            ss_attn = ss_attn + jnp.sum(og * og, axis=-1, keepdims=True)
            c0 = mlp_w + (h * q_per_kv + g) * HEAD_DIM
            ybuf[:, c0:c0 + HEAD_DIM] = og

    rs_mlp = lax.rsqrt(ss_mlp * np.float32(1.0 / mlp_w) + EPS)
    rs_attn = lax.rsqrt(ss_attn * np.float32(1.0 / attn_w) + EPS)
    o_ref[:, :mlp_w] = (ybuf[:, :mlp_w] * rs_mlp * gmix_ref[:, :mlp_w]).astype(BF16)
    o_ref[:, mlp_w:] = (ybuf[:, mlp_w:] * rs_attn * gmix_ref[:, mlp_w:]).astype(BF16)


def _mixer(proj, kvctx, w_s, b_sT, sink, g_mix, batch, seq, n_groups, n_kv, q_per_kv, c_len):
    nb = seq // BLOCK
    mlp_w = n_groups * HEAD_DIM
    attn_w = n_kv * q_per_kv * HEAD_DIM
    kv_w = n_kv * HEAD_DIM
    k_col = (2 * mlp_w + attn_w) // kv_w
    v_col = k_col + 1
    row = lambda b, i: b * nb + i
    prev = lambda b, i: b * nb + jnp.maximum(i - 1, 0)
    nxt = lambda b, i: b * nb + jnp.minimum(i + 1, nb - 1)
    wide = lambda c: pl.BlockSpec((BLOCK, mlp_w), lambda b, i: (row(b, i), c))
    kvs = lambda rfn, c: pl.BlockSpec((BLOCK, kv_w), lambda b, i: (rfn(b, i), c))
    return pl.pallas_call(
        functools.partial(_mixer_kernel, n_groups=n_groups, n_kv=n_kv, q_per_kv=q_per_kv, c_len=c_len, seq=seq),
        grid=(batch, nb),
        in_specs=[pl.BlockSpec(memory_space=pltpu.SMEM),
                  wide(0), wide(1), wide(2),
                  kvs(prev, k_col), kvs(row, k_col), kvs(nxt, k_col),
                  kvs(prev, v_col), kvs(row, v_col), kvs(nxt, v_col),
                  pl.BlockSpec((c_len, kv_w), lambda b, i: (b, 0)),
                  pl.BlockSpec((c_len, kv_w), lambda b, i: (b, 1)),
                  pl.BlockSpec((n_groups, CHUNK, CHUNK), lambda b, i: (0, 0, 0)),
                  pl.BlockSpec((CHUNK, n_groups), lambda b, i: (0, 0)),
                  pl.BlockSpec((1, mlp_w + attn_w), lambda b, i: (0, 0))],
        out_specs=pl.BlockSpec((BLOCK, mlp_w + attn_w), lambda b, i: (row(b, i), 0)),
        out_shape=jax.ShapeDtypeStruct((batch * seq, mlp_w + attn_w), BF16),
        scratch_shapes=[pltpu.VMEM((BLOCK, mlp_w + attn_w), F32)],
        compiler_params=_cparams(("arbitrary", "arbitrary")),
        name="mixer_core",
    )(sink, proj, proj, proj, proj, proj, proj, proj, proj, proj, kvctx, kvctx, w_s, b_sT, g_mix)


def _rope_tables(n):
    rows = n // GRID_W
    row = jnp.broadcast_to(jnp.arange(rows, dtype=F32)[:, None], (rows, GRID_W)).reshape(n)
    col = jnp.broadcast_to(jnp.arange(GRID_W, dtype=F32)[None, :], (rows, GRID_W)).reshape(n)
    n_freq = HEAD_DIM // 4
    inv_freq = ROPE_BASE ** (-jnp.arange(n_freq, dtype=F32) / n_freq)
    ang = jnp.concatenate([row[:, None] * inv_freq, col[:, None] * inv_freq], axis=-1)
    cs, sn = jnp.cos(ang), jnp.sin(ang)
    return jnp.concatenate([cs, cs], axis=-1), jnp.concatenate([-sn, sn], axis=-1)


def _pad_cols(w, n):
    return jnp.pad(w, ((0, 0), (0, n - w.shape[1])))


def _pad_rows(w, n):
    return jnp.pad(w, ((0, n - w.shape[0]), (0, 0)))


def kernel(x, c, ctx, c_ctx, w_ada, b_ada, w_ffn_gate, w_ffn_up, w_ffn_down, w_in, w_spatial, b_spatial,
           sink_logit, g_mix, w_out, ln_gain, ln_bias):
    b, n, d = x.shape
    c_len = ctx.shape[1]
    depth = w_ada.shape[0]
    d_ff = w_ffn_gate.shape[-1]
    n_groups = w_spatial.shape[1]
    mlp_w = n_groups * HEAD_DIM
    n_q = sink_logit.shape[1]
    attn_w = n_q * HEAD_DIM
    kv_w = (w_in.shape[-1] - 2 * mlp_w - attn_w) // 2
    n_kv = kv_w // HEAD_DIM
    q_per_kv = n_q // n_kv
    assert depth == 1 and b + 1 <= 8
    alpha = float((2.0 * depth) ** 0.25)
    ff_pad = -(-d_ff // FF_PAD_MULTIPLE) * FF_PAD_MULTIPLE
    ctx_seg = b

    x_seg = lambda r0: r0 // n
    c_seg = lambda r0: ctx_seg

    cosf, sinf = _rope_tables(n)
    x2 = x.reshape(b * n, d)
    ctx2 = ctx.reshape(b * c_len, d)

    layer = 0
    c8 = jnp.concatenate([c, c_ctx[None, :], jnp.zeros((8 - b - 1, d), F32)], axis=0)
    m = _ada(c8, w_ada[layer], b_ada[layer]).reshape(8, N_SUB, N_MOD, d)
    mod = lambda s, k: m[:, s, k, :][:, None, :]

    def ffn(h, hm, idx, s, gain, bias, seg_fn, next_mod, tm_up):
        wg = _pad_cols(w_ffn_gate[layer, idx].astype(BF16), ff_pad)
        wu = _pad_cols(w_ffn_up[layer, idx].astype(BF16), ff_pad)
        wd = _pad_rows(w_ffn_down[layer, idx].astype(BF16), ff_pad)
        a = _swiglu_up(hm, wg, wu, tm=tm_up, tn=512)
        y = _matmul(a, wd, F32, tm=tm_up, tn=1024, tk=ff_pad // 4, name="ffn_down")
        return _resid_ln(h, y, mod(s, 2), gain, bias, seg_fn, alpha, 0.5, next_mod)

    xm = _modcast(x2, mod(0, 0), mod(0, 1), x_seg)
    cm = _modcast(ctx2, mod(0, 0), mod(0, 1), c_seg)
    mix_mod = (mod(1, 0), mod(1, 1))
    x1, hx = ffn(x2, xm, 0, 0, ln_gain[layer, 0], ln_bias[layer, 0], x_seg, mix_mod, 1024)
    _, hc = ffn(ctx2, cm, 0, 0, ln_gain[layer, 0], ln_bias[layer, 0], c_seg, mix_mod, 1024)

    w_in_b = w_in[layer].astype(BF16)
    proj = _proj(hx, w_in_b, cosf, sinf, n, 2 * mlp_w, attn_w + kv_w)
    kvctx = _matmul(hc, w_in_b[:, 2 * mlp_w + attn_w:], BF16, tm=b * c_len, tn=2 * kv_w, tk=d, name="ctx_kv")
    ymix = _mixer(proj, kvctx, w_spatial[layer].astype(BF16), b_spatial[layer].T, sink_logit[layer],
                  g_mix[layer].reshape(1, mlp_w + attn_w), b, n, n_groups, n_kv, q_per_kv, c_len)
    mix = _matmul(ymix, w_out[layer].astype(BF16), F32, tm=1024, tn=1024, tk=mlp_w + attn_w, name="out_proj")
    x_mid, hm2 = _resid_ln(x1, mix, mod(1, 2), ln_gain[layer, 1], ln_bias[layer, 1], x_seg, alpha, 1.0,
                           (mod(2, 0), mod(2, 1)))

    out, _ = ffn(x_mid, hm2, 1, 2, ln_gain[layer, 2], ln_bias[layer, 2], x_seg, None, 1024)
    return out.reshape(b, n, d)
```

```python
import functools

import jax
import jax.numpy as jnp
import numpy as np
from jax import lax
from jax.experimental import pallas as pl
from jax.experimental.pallas import tpu as pltpu

F32 = jnp.float32
BF16 = jnp.bfloat16

HEAD_DIM = 128
CHUNK = 128
WINDOW = 128
BLOCK = 128
GRID_W = 64
ROPE_BASE = 10000.0
EPS = 1e-6
N_SUB = 3
N_MOD = 3

VMEM_LIMIT_BYTES = 58 * 1024 * 1024
FF_TILE = 256
UP_TM = 2048


def _cparams(sem):
    return pltpu.CompilerParams(dimension_semantics=sem, vmem_limit_bytes=VMEM_LIMIT_BYTES)


def _ada_kernel(c_ref, w_ref, b_ref, o_ref):
    c = c_ref[...]
    s = (c * jax.nn.sigmoid(c)).astype(BF16)
    o_ref[...] = jnp.dot(s, w_ref[...].astype(BF16), preferred_element_type=F32) + b_ref[...]


def _ada(c8, w_ada, b_ada, tn=512):
    d, n = w_ada.shape
    return pl.pallas_call(
        _ada_kernel,
        grid=(n // tn,),
        in_specs=[pl.BlockSpec((8, d), lambda j: (0, 0)),
                  pl.BlockSpec((d, tn), lambda j: (0, j)),
                  pl.BlockSpec((1, tn), lambda j: (0, j))],
        out_specs=pl.BlockSpec((8, tn), lambda j: (0, j)),
        out_shape=jax.ShapeDtypeStruct((8, n), F32),
        compiler_params=_cparams(("arbitrary",)),
        name="ada",
    )(c8, w_ada, b_ada.reshape(1, n))


def _modcast_kernel(x_ref, shift_ref, scale_ref, o_ref):
    o_ref[...] = (x_ref[...] * (1.0 + scale_ref[...]) + shift_ref[...]).astype(BF16)


def _modcast(x2d, shift, scale, seg_fn, tm=256):
    r, d = x2d.shape
    vec = pl.BlockSpec((None, 1, d), lambda i: (seg_fn(i * tm), 0, 0))
    return pl.pallas_call(
        _modcast_kernel,
        grid=(r // tm,),
        in_specs=[pl.BlockSpec((tm, d), lambda i: (i, 0)), vec, vec],
        out_specs=pl.BlockSpec((tm, d), lambda i: (i, 0)),
        out_shape=jax.ShapeDtypeStruct((r, d), BF16),
        compiler_params=_cparams(("arbitrary",)),
        name="modcast",
    )(x2d, shift, scale)


def _swiglu_up_kernel(x_ref, wg_ref, wu_ref, o_ref):
    x = x_ref[...]
    g = jnp.dot(x, wg_ref[...].astype(BF16), preferred_element_type=F32)
    u = jnp.dot(x, wu_ref[...].astype(BF16), preferred_element_type=F32)
    o_ref[...] = (g * jax.nn.sigmoid(g) * u).astype(BF16)


def _swiglu_up(xm, w_gate, w_up, lead, tm, tn):
    r, d = xm.shape
    f = w_gate.shape[-1]
    wspec = pl.BlockSpec((None,) * len(lead) + (d, tn), lambda i, j: lead + (0, j))
    return pl.pallas_call(
        _swiglu_up_kernel,
        grid=(r // tm, f // tn),
        in_specs=[pl.BlockSpec((tm, d), lambda i, j: (i, 0)), wspec, wspec],
        out_specs=pl.BlockSpec((tm, tn), lambda i, j: (i, j)),
        out_shape=jax.ShapeDtypeStruct((r, f), BF16),
        compiler_params=_cparams(("arbitrary", "arbitrary")),
        name="swiglu_up",
    )(xm, w_gate, w_up)


def _mm_kernel(a_ref, b_ref, o_ref, *, nk):
    d = jnp.dot(a_ref[...], b_ref[...].astype(BF16), preferred_element_type=F32)
    if nk == 1:
        o_ref[...] = d.astype(o_ref.dtype)
    else:
        k = pl.program_id(2)

        @pl.when(k == 0)
        def _():
            o_ref[...] = d

        @pl.when(k > 0)
        def _():
            o_ref[...] += d


def _matmul(a, b, lead, out_dtype, tm, tn, tk, name, col0=0, n=None):
    r, kk = a.shape
    n = b.shape[-1] - col0 if n is None else n
    nk = kk // tk
    assert nk * tk == kk and col0 % tn == 0 and n % tn == 0
    assert nk == 1 or out_dtype == F32
    return pl.pallas_call(
        functools.partial(_mm_kernel, nk=nk),
        grid=(r // tm, n // tn, nk),
        in_specs=[pl.BlockSpec((tm, tk), lambda i, j, k: (i, k)),
                  pl.BlockSpec((None,) * len(lead) + (tk, tn), lambda i, j, k: lead + (k, j + col0 // tn))],
        out_specs=pl.BlockSpec((tm, tn), lambda i, j, k: (i, j)),
        out_shape=jax.ShapeDtypeStruct((r, n), out_dtype),
        compiler_params=_cparams(("arbitrary", "arbitrary", "arbitrary")),
        name=name,
    )(a, b)


def _resid_ln_kernel(*refs, alpha, coef, emit_mod):
    if emit_mod:
        h_ref, y_ref, gate_ref, gain_ref, bias_ref, shift_ref, scale_ref, o_ref, om_ref = refs
    else:
        h_ref, y_ref, gate_ref, gain_ref, bias_ref, o_ref = refs
    z = alpha * h_ref[...] + (coef * gate_ref[...]) * y_ref[...]
    mu = jnp.mean(z, axis=-1, keepdims=True)
    zc = z - mu
    var = jnp.mean(zc * zc, axis=-1, keepdims=True)
    hn = zc * lax.rsqrt(var + EPS) * gain_ref[...] + bias_ref[...]
    o_ref[...] = hn
    if emit_mod:
        om_ref[...] = (hn * (1.0 + scale_ref[...]) + shift_ref[...]).astype(BF16)


def _resid_ln(h, y, gate, gain, bias, seg_fn, alpha, coef, next_mod=None, tm=256):
    r, d = h.shape
    row = pl.BlockSpec((tm, d), lambda i: (i, 0))
    vec = pl.BlockSpec((None, 1, d), lambda i: (seg_fn(i * tm), 0, 0))
    full = pl.BlockSpec((1, d), lambda i: (0, 0))
    emit_mod = next_mod is not None
    in_specs = [row, row, vec, full, full]
    args = [h, y, gate, gain.reshape(1, d), bias.reshape(1, d)]
    out_specs = [row]
    out_shape = [jax.ShapeDtypeStruct((r, d), F32)]
    if emit_mod:
        in_specs += [vec, vec]
        args += list(next_mod)
        out_specs.append(row)
        out_shape.append(jax.ShapeDtypeStruct((r, d), BF16))
    outs = pl.pallas_call(
        functools.partial(_resid_ln_kernel, alpha=alpha, coef=coef, emit_mod=emit_mod),
        grid=(r // tm,),
        in_specs=in_specs,
        out_specs=out_specs,
        out_shape=out_shape,
        compiler_params=_cparams(("arbitrary",)),
        name="resid_ln",
    )(*args)
    return outs if emit_mod else (outs[0], None)


def _proj_kernel(x_ref, w_ref, cos_ref, sin_ref, o_ref, *, n_gelu, n_rope, heads_per_tile):
    j = pl.program_id(1)
    acc = jnp.dot(x_ref[...], w_ref[...].astype(BF16), preferred_element_type=F32)

    @pl.when(j < n_gelu)
    def _():
        o_ref[...] = (0.5 * acc * (1.0 + lax.erf(acc * np.float32(np.sqrt(0.5))))).astype(BF16)

    @pl.when((j >= n_gelu) & (j < n_gelu + n_rope))
    def _():
        cs = cos_ref[...]
        sn = sin_ref[...]
        for hh in range(heads_per_tile):
            t = acc[:, hh * HEAD_DIM:(hh + 1) * HEAD_DIM]
            rot = pltpu.roll(t, HEAD_DIM // 2, 1)
            o_ref[:, hh * HEAD_DIM:(hh + 1) * HEAD_DIM] = (t * cs + rot * sn).astype(BF16)

    @pl.when(j >= n_gelu + n_rope)
    def _():
        o_ref[...] = acc.astype(BF16)


def _proj(hx, w_in, lead, cosf, sinf, seq, n_gelu_cols, n_rope_cols, tm=1024, tn=512):
    r, d = hx.shape
    n = w_in.shape[-1]
    rows_per_seq = seq // tm
    return pl.pallas_call(
        functools.partial(_proj_kernel, n_gelu=n_gelu_cols // tn, n_rope=n_rope_cols // tn,
                          heads_per_tile=tn // HEAD_DIM),
        grid=(r // tm, n // tn),
        in_specs=[pl.BlockSpec((tm, d), lambda i, j: (i, 0)),
                  pl.BlockSpec((None,) * len(lead) + (d, tn), lambda i, j: lead + (0, j)),
                  pl.BlockSpec((tm, HEAD_DIM), lambda i, j: (i % rows_per_seq, 0)),
                  pl.BlockSpec((tm, HEAD_DIM), lambda i, j: (i % rows_per_seq, 0))],
        out_specs=pl.BlockSpec((tm, tn), lambda i, j: (i, j)),
        out_shape=jax.ShapeDtypeStruct((r, n), BF16),
        compiler_params=_cparams(("arbitrary", "arbitrary")),
        name="in_proj",
    )(hx, w_in, cosf, sinf)


def _mixer_kernel(sink_ref, gu_ref, gv_ref, q_ref, kp_ref, kc_ref, kn_ref, vp_ref, vc_ref, vn_ref,
                  kctx_ref, vctx_ref, ws_ref, bs_ref, gmix_ref, o_ref, ybuf,
                  *, n_groups, n_kv, q_per_kv, c_len, seq):
    blk = pl.program_id(1)
    mlp_w = n_groups * HEAD_DIM
    attn_w = n_kv * q_per_kv * HEAD_DIM

    gv = gv_ref[...].astype(F32)
    mu = jnp.mean(gv, axis=-1, keepdims=True)
    vc_ = gv - mu
    var = jnp.mean(vc_ * vc_, axis=-1, keepdims=True)
    vstd = (vc_ * lax.rsqrt(var + EPS)).astype(BF16)
    ss_mlp = jnp.zeros((CHUNK, 1), F32)
    for g in range(n_groups):
        sl = slice(g * HEAD_DIM, (g + 1) * HEAD_DIM)
        mixed = jnp.dot(ws_ref[g], vstd[:, sl], preferred_element_type=F32) + bs_ref[:, g:g + 1]
        ym = gu_ref[:, sl].astype(F32) * mixed
        ss_mlp = ss_mlp + jnp.sum(ym * ym, axis=-1, keepdims=True)
        ybuf[:, sl] = ym

    n_keys = c_len + 3 * BLOCK
    rows = q_per_kv * BLOCK
    r_i = lax.broadcasted_iota(jnp.int32, (rows, n_keys), 0) & (BLOCK - 1)
    col = lax.broadcasted_iota(jnp.int32, (rows, n_keys), 1)
    k_off = col - (c_len + BLOCK)
    kpos = blk * BLOCK + k_off
    valid = (col < c_len) | ((jnp.abs(k_off - r_i) <= WINDOW) & (kpos >= 0) & (kpos < seq))
    scale = np.float32(HEAD_DIM ** -0.5)
    ss_attn = jnp.zeros((BLOCK, 1), F32)
    for h in range(n_kv):
        hs = slice(h * HEAD_DIM, (h + 1) * HEAD_DIM)
        q4 = jnp.concatenate(
            [q_ref[:, (h * q_per_kv + g) * HEAD_DIM:(h * q_per_kv + g + 1) * HEAD_DIM] for g in range(q_per_kv)],
            axis=0)
        keys = jnp.concatenate([kctx_ref[:, hs], kp_ref[:, hs], kc_ref[:, hs], kn_ref[:, hs]], axis=0)
        vals = jnp.concatenate([vctx_ref[:, hs], vp_ref[:, hs], vc_ref[:, hs], vn_ref[:, hs]], axis=0)
        s = lax.dot_general(q4, keys, (((1,), (1,)), ((), ())), preferred_element_type=F32) * scale
        s = jnp.where(valid, s, -jnp.inf)
        sink_col = jnp.concatenate(
            [jnp.full((BLOCK, 1), sink_ref[h * q_per_kv + g], F32) for g in range(q_per_kv)], axis=0)
        m = jnp.maximum(jnp.max(s, axis=-1, keepdims=True), sink_col)
        p = jnp.exp(s - m)
        denom = jnp.sum(p, axis=-1, keepdims=True) + jnp.exp(sink_col - m)
        o = jnp.dot(p.astype(BF16), vals, preferred_element_type=F32) * (1.0 / denom)
        for g in range(q_per_kv):
            og = o[g * BLOCK:(g + 1) * BLOCK, :]
            ss_attn = ss_attn + jnp.sum(og * og, axis=-1, keepdims=True)
            c0 = mlp_w + (h * q_per_kv + g) * HEAD_DIM
            ybuf[:, c0:c0 + HEAD_DIM] = og

    rs_mlp = lax.rsqrt(ss_mlp * np.float32(1.0 / mlp_w) + EPS)
    rs_attn = lax.rsqrt(ss_attn * np.float32(1.0 / attn_w) + EPS)
    o_ref[:, :mlp_w] = (ybuf[:, :mlp_w] * rs_mlp * gmix_ref[:, :mlp_w]).astype(BF16)
    o_ref[:, mlp_w:] = (ybuf[:, mlp_w:] * rs_attn * gmix_ref[:, mlp_w:]).astype(BF16)


def _mixer(proj, kvctx, w_s, b_sT, sink, g_mix, batch, seq, n_groups, n_kv, q_per_kv, c_len):
    nb = seq // BLOCK
    mlp_w = n_groups * HEAD_DIM
    attn_w = n_kv * q_per_kv * HEAD_DIM
    kv_w = n_kv * HEAD_DIM
    k_col = (2 * mlp_w + attn_w) // kv_w
    v_col = k_col + 1
    row = lambda b, i: b * nb + i
    prev = lambda b, i: b * nb + jnp.maximum(i - 1, 0)
    nxt = lambda b, i: b * nb + jnp.minimum(i + 1, nb - 1)
    wide = lambda c: pl.BlockSpec((BLOCK, mlp_w), lambda b, i: (row(b, i), c))
    kvs = lambda rfn, c: pl.BlockSpec((BLOCK, kv_w), lambda b, i: (rfn(b, i), c))
    return pl.pallas_call(
        functools.partial(_mixer_kernel, n_groups=n_groups, n_kv=n_kv, q_per_kv=q_per_kv, c_len=c_len, seq=seq),
        grid=(batch, nb),
        in_specs=[pl.BlockSpec(memory_space=pltpu.SMEM),
                  wide(0), wide(1), wide(2),
                  kvs(prev, k_col), kvs(row, k_col), kvs(nxt, k_col),
                  kvs(prev, v_col), kvs(row, v_col), kvs(nxt, v_col),
                  pl.BlockSpec((c_len, kv_w), lambda b, i: (b, 0)),
                  pl.BlockSpec((c_len, kv_w), lambda b, i: (b, 1)),
                  pl.BlockSpec((n_groups, CHUNK, CHUNK), lambda b, i: (0, 0, 0)),
                  pl.BlockSpec((CHUNK, n_groups), lambda b, i: (0, 0)),
                  pl.BlockSpec((1, mlp_w + attn_w), lambda b, i: (0, 0))],
        out_specs=pl.BlockSpec((BLOCK, mlp_w + attn_w), lambda b, i: (row(b, i), 0)),
        out_shape=jax.ShapeDtypeStruct((batch * seq, mlp_w + attn_w), BF16),
        scratch_shapes=[pltpu.VMEM((BLOCK, mlp_w + attn_w), F32)],
        compiler_params=_cparams(("arbitrary", "arbitrary")),
        name="mixer_core",
    )(sink, proj, proj, proj, proj, proj, proj, proj, proj, proj, kvctx, kvctx, w_s, b_sT, g_mix)


def _rope_tables(n):
    rows = n // GRID_W
    row = jnp.broadcast_to(jnp.arange(rows, dtype=F32)[:, None], (rows, GRID_W)).reshape(n)
    col = jnp.broadcast_to(jnp.arange(GRID_W, dtype=F32)[None, :], (rows, GRID_W)).reshape(n)
    n_freq = HEAD_DIM // 4
    inv_freq = ROPE_BASE ** (-jnp.arange(n_freq, dtype=F32) / n_freq)
    ang = jnp.concatenate([row[:, None] * inv_freq, col[:, None] * inv_freq], axis=-1)
    cs, sn = jnp.cos(ang), jnp.sin(ang)
    return jnp.concatenate([cs, cs], axis=-1), jnp.concatenate([-sn, sn], axis=-1)


def kernel(x, c, ctx, c_ctx, w_ada, b_ada, w_ffn_gate, w_ffn_up, w_ffn_down, w_in, w_spatial, b_spatial,
           sink_logit, g_mix, w_out, ln_gain, ln_bias):
    b, n, d = x.shape
    c_len = ctx.shape[1]
    depth = w_ada.shape[0]
    d_ff = w_ffn_gate.shape[-1]
    n_groups = w_spatial.shape[1]
    mlp_w = n_groups * HEAD_DIM
    n_q = sink_logit.shape[1]
    attn_w = n_q * HEAD_DIM
    kv_w = (w_in.shape[-1] - 2 * mlp_w - attn_w) // 2
    n_kv = kv_w // HEAD_DIM
    q_per_kv = n_q // n_kv
    assert depth == 1 and b + 1 <= 8
    alpha = float((2.0 * depth) ** 0.25)
    ctx_seg = b

    x_seg = lambda r0: r0 // n
    c_seg = lambda r0: ctx_seg

    cosf, sinf = _rope_tables(n)
    x2 = x.reshape(b * n, d)
    ctx2 = ctx.reshape(b * c_len, d)

    layer = 0
    c8 = jnp.concatenate([c, c_ctx[None, :], jnp.zeros((8 - b - 1, d), F32)], axis=0)
    m = _ada(c8, w_ada[layer], b_ada[layer]).reshape(8, N_SUB, N_MOD, d)
    mod = lambda s, k: m[:, s, k, :][:, None, :]

    w_down_b = w_ffn_down.astype(BF16)

    def ffn(h, hm, idx, s, gain, bias, seg_fn, next_mod, tm_up):
        a = _swiglu_up(hm, w_ffn_gate, w_ffn_up, (layer, idx), tm=tm_up, tn=FF_TILE)
        y = _matmul(a, w_down_b, (layer, idx), F32, tm=1024, tn=512, tk=d_ff // 2, name="ffn_down")
        return _resid_ln(h, y, mod(s, 2), gain, bias, seg_fn, alpha, 0.5, next_mod)

    xm = _modcast(x2, mod(0, 0), mod(0, 1), x_seg)
    cm = _modcast(ctx2, mod(0, 0), mod(0, 1), c_seg)
    mix_mod = (mod(1, 0), mod(1, 1))
    x1, hx = ffn(x2, xm, 0, 0, ln_gain[layer, 0], ln_bias[layer, 0], x_seg, mix_mod, UP_TM)
    _, hc = ffn(ctx2, cm, 0, 0, ln_gain[layer, 0], ln_bias[layer, 0], c_seg, mix_mod, 1024)

    proj = _proj(hx, w_in, (layer,), cosf, sinf, n, 2 * mlp_w, attn_w + kv_w)
    kvctx = _matmul(hc, w_in, (layer,), BF16, tm=b * c_len, tn=kv_w, tk=d, name="ctx_kv",
                    col0=2 * mlp_w + attn_w)
    ymix = _mixer(proj, kvctx, w_spatial[layer].astype(BF16), b_spatial[layer].T, sink_logit[layer],
                  g_mix[layer].reshape(1, mlp_w + attn_w), b, n, n_groups, n_kv, q_per_kv, c_len)
    mix = _matmul(ymix, w_out, (layer,), F32, tm=1024, tn=512, tk=mlp_w + attn_w, name="out_proj")
    x_mid, hm2 = _resid_ln(x1, mix, mod(1, 2), ln_gain[layer, 1], ln_bias[layer, 1], x_seg, alpha, 1.0,
                           (mod(2, 0), mod(2, 1)))

    out, _ = ffn(x_mid, hm2, 1, 2, ln_gain[layer, 2], ln_bias[layer, 2], x_seg, None, UP_TM)
    return out.reshape(b, n, d)
```

```python
import functools

import jax
import jax.numpy as jnp
import numpy as np
from jax import lax
from jax.experimental import pallas as pl
from jax.experimental.pallas import tpu as pltpu

F32 = jnp.float32
BF16 = jnp.bfloat16

HEAD_DIM = 128
CHUNK = 128
WINDOW = 128
BLOCK = 128
GRID_W = 64
ROPE_BASE = 10000.0
EPS = 1e-6
N_SUB = 3
N_MOD = 3

VMEM_LIMIT_BYTES = 58 * 1024 * 1024
FF_TILE = 256
UP_TM = 2048


def _cparams(sem):
    return pltpu.CompilerParams(dimension_semantics=sem, vmem_limit_bytes=VMEM_LIMIT_BYTES)


def _ada_kernel(c_ref, w_ref, b_ref, o_ref):
    c = c_ref[...]
    s = (c * jax.nn.sigmoid(c)).astype(BF16)
    o_ref[...] = jnp.dot(s, w_ref[...].astype(BF16), preferred_element_type=F32) + b_ref[...]


def _ada(c8, w_ada, b_ada, tn=512):
    d, n = w_ada.shape
    return pl.pallas_call(
        _ada_kernel,
        grid=(n // tn,),
        in_specs=[pl.BlockSpec((8, d), lambda j: (0, 0)),
                  pl.BlockSpec((d, tn), lambda j: (0, j)),
                  pl.BlockSpec((1, tn), lambda j: (0, j))],
        out_specs=pl.BlockSpec((8, tn), lambda j: (0, j)),
        out_shape=jax.ShapeDtypeStruct((8, n), F32),
        compiler_params=_cparams(("arbitrary",)),
        name="ada",
    )(c8, w_ada, b_ada.reshape(1, n))


def _modcast_kernel(x_ref, shift_ref, scale_ref, o_ref):
    o_ref[...] = (x_ref[...] * (1.0 + scale_ref[...]) + shift_ref[...]).astype(BF16)


def _modcast(x2d, shift, scale, seg_fn, tm=256):
    r, d = x2d.shape
    vec = pl.BlockSpec((None, 1, d), lambda i: (seg_fn(i * tm), 0, 0))
    return pl.pallas_call(
        _modcast_kernel,
        grid=(r // tm,),
        in_specs=[pl.BlockSpec((tm, d), lambda i: (i, 0)), vec, vec],
        out_specs=pl.BlockSpec((tm, d), lambda i: (i, 0)),
        out_shape=jax.ShapeDtypeStruct((r, d), BF16),
        compiler_params=_cparams(("arbitrary",)),
        name="modcast",
    )(x2d, shift, scale)


def _swiglu_up_kernel(x_ref, wg_ref, wu_ref, o_ref):
    x = x_ref[...]
    g = jnp.dot(x, wg_ref[...].astype(BF16), preferred_element_type=F32)
    u = jnp.dot(x, wu_ref[...].astype(BF16), preferred_element_type=F32)
    o_ref[...] = (g * jax.nn.sigmoid(g) * u).astype(BF16)


def _swiglu_up(xm, w_gate, w_up, lead, tm, tn):
    r, d = xm.shape
    f = w_gate.shape[-1]
    wspec = pl.BlockSpec((None,) * len(lead) + (d, tn), lambda i, j: lead + (0, j))
    return pl.pallas_call(
        _swiglu_up_kernel,
        grid=(r // tm, f // tn),
        in_specs=[pl.BlockSpec((tm, d), lambda i, j: (i, 0)), wspec, wspec],
        out_specs=pl.BlockSpec((tm, tn), lambda i, j: (i, j)),
        out_shape=jax.ShapeDtypeStruct((r, f), BF16),
        compiler_params=_cparams(("arbitrary", "arbitrary")),
        name="swiglu_up",
    )(xm, w_gate, w_up)


def _mm_kernel(a_ref, b_ref, o_ref, *, nk):
    d = jnp.dot(a_ref[...], b_ref[...].astype(BF16), preferred_element_type=F32)
    if nk == 1:
        o_ref[...] = d.astype(o_ref.dtype)
    else:
        k = pl.program_id(2)

        @pl.when(k == 0)
        def _():
            o_ref[...] = d

        @pl.when(k > 0)
        def _():
            o_ref[...] += d


def _matmul(a, b, lead, out_dtype, tm, tn, tk, name, col0=0, n=None):
    r, kk = a.shape
    n = b.shape[-1] - col0 if n is None else n
    nk = kk // tk
    assert nk * tk == kk and col0 % tn == 0 and n % tn == 0
    assert nk == 1 or out_dtype == F32
    return pl.pallas_call(
        functools.partial(_mm_kernel, nk=nk),
        grid=(r // tm, n // tn, nk),
        in_specs=[pl.BlockSpec((tm, tk), lambda i, j, k: (i, k)),
                  pl.BlockSpec((None,) * len(lead) + (tk, tn), lambda i, j, k: lead + (k, j + col0 // tn))],
        out_specs=pl.BlockSpec((tm, tn), lambda i, j, k: (i, j)),
        out_shape=jax.ShapeDtypeStruct((r, n), out_dtype),
        compiler_params=_cparams(("arbitrary", "arbitrary", "arbitrary")),
        name=name,
    )(a, b)


def _resid_ln_kernel(*refs, alpha, coef, emit_mod):
    if emit_mod:
        h_ref, y_ref, gate_ref, gain_ref, bias_ref, shift_ref, scale_ref, o_ref, om_ref = refs
    else:
        h_ref, y_ref, gate_ref, gain_ref, bias_ref, o_ref = refs
    z = alpha * h_ref[...] + (coef * gate_ref[...]) * y_ref[...]
    mu = jnp.mean(z, axis=-1, keepdims=True)
    zc = z - mu
    var = jnp.mean(zc * zc, axis=-1, keepdims=True)
    hn = zc * lax.rsqrt(var + EPS) * gain_ref[...] + bias_ref[...]
    o_ref[...] = hn
    if emit_mod:
        om_ref[...] = (hn * (1.0 + scale_ref[...]) + shift_ref[...]).astype(BF16)


def _resid_ln(h, y, gate, gain, bias, seg_fn, alpha, coef, next_mod=None, tm=256):
    r, d = h.shape
    row = pl.BlockSpec((tm, d), lambda i: (i, 0))
    vec = pl.BlockSpec((None, 1, d), lambda i: (seg_fn(i * tm), 0, 0))
    full = pl.BlockSpec((1, d), lambda i: (0, 0))
    emit_mod = next_mod is not None
    in_specs = [row, row, vec, full, full]
    args = [h, y, gate, gain.reshape(1, d), bias.reshape(1, d)]
    out_specs = [row]
    out_shape = [jax.ShapeDtypeStruct((r, d), F32)]
    if emit_mod:
        in_specs += [vec, vec]
        args += list(next_mod)
        out_specs.append(row)
        out_shape.append(jax.ShapeDtypeStruct((r, d), BF16))
    outs = pl.pallas_call(
        functools.partial(_resid_ln_kernel, alpha=alpha, coef=coef, emit_mod=emit_mod),
        grid=(r // tm,),
        in_specs=in_specs,
        out_specs=out_specs,
        out_shape=out_shape,
        compiler_params=_cparams(("arbitrary",)),
        name="resid_ln",
    )(*args)
    return outs if emit_mod else (outs[0], None)


def _proj_gelu_kernel(x_ref, w_ref, o_ref):
    acc = jnp.dot(x_ref[...], w_ref[...].astype(BF16), preferred_element_type=F32)
    o_ref[...] = (0.5 * acc * (1.0 + lax.erf(acc * np.float32(np.sqrt(0.5))))).astype(BF16)


def _proj_rope_kernel(x_ref, w_ref, cos_ref, sin_ref, o_ref, *, heads_per_tile):
    acc = jnp.dot(x_ref[...], w_ref[...].astype(BF16), preferred_element_type=F32)
    cs = cos_ref[...]
    sn = sin_ref[...]
    for hh in range(heads_per_tile):
        t = acc[:, hh * HEAD_DIM:(hh + 1) * HEAD_DIM]
        rot = pltpu.roll(t, HEAD_DIM // 2, 1)
        o_ref[:, hh * HEAD_DIM:(hh + 1) * HEAD_DIM] = (t * cs + rot * sn).astype(BF16)


def _proj(hx, w_in, lead, col0, n, rope=None, tm=1024, tn=512):
    r, d = hx.shape
    assert col0 % tn == 0 and n % tn == 0
    in_specs = [pl.BlockSpec((tm, d), lambda i, j: (i, 0)),
                pl.BlockSpec((None,) * len(lead) + (d, tn), lambda i, j: lead + (0, j + col0 // tn))]
    args = [hx, w_in]
    if rope is None:
        body, name = _proj_gelu_kernel, "in_proj_gelu"
    else:
        cosf, sinf, seq = rope
        tiles_per_seq = seq // tm
        tab = pl.BlockSpec((tm, HEAD_DIM), lambda i, j: (i % tiles_per_seq, 0))
        in_specs += [tab, tab]
        args += [cosf, sinf]
        body, name = functools.partial(_proj_rope_kernel, heads_per_tile=tn // HEAD_DIM), "in_proj_rope"
    return pl.pallas_call(
        body,
        grid=(r // tm, n // tn),
        in_specs=in_specs,
        out_specs=pl.BlockSpec((tm, tn), lambda i, j: (i, j)),
        out_shape=jax.ShapeDtypeStruct((r, n), BF16),
        compiler_params=_cparams(("arbitrary", "arbitrary")),
        name=name,
    )(*args)


def _mixer_kernel(sink_ref, gu_ref, gv_ref, q_ref, kp_ref, kc_ref, kn_ref, vp_ref, vc_ref, vn_ref,
                  kctx_ref, vctx_ref, ws_ref, bs_ref, gmix_ref, band_ref, edge_ref, o_ref, ybuf,
                  *, n_groups, n_kv, q_per_kv, seq):
    blk = pl.program_id(1)
    mlp_w = n_groups * HEAD_DIM
    attn_w = n_kv * q_per_kv * HEAD_DIM

    gv = gv_ref[...].astype(F32)
    mu = jnp.mean(gv, axis=-1, keepdims=True)
    vc_ = gv - mu
    var = jnp.mean(vc_ * vc_, axis=-1, keepdims=True)
    vstd = (vc_ * lax.rsqrt(var + EPS)).astype(BF16)
    ss_mlp = jnp.zeros((CHUNK, 1), F32)
    for g in range(n_groups):
        sl = slice(g * HEAD_DIM, (g + 1) * HEAD_DIM)
        mixed = jnp.dot(ws_ref[g], vstd[:, sl], preferred_element_type=F32) + bs_ref[:, g:g + 1]
        ym = gu_ref[:, sl].astype(F32) * mixed
        ss_mlp = ss_mlp + jnp.sum(ym * ym, axis=-1, keepdims=True)
        ybuf[:, sl] = ym

    edge = (jnp.where(blk == 0, edge_ref[0:1, :], 0.0)
            + jnp.where(blk == seq // BLOCK - 1, edge_ref[1:2, :], 0.0))
    bias = band_ref[...] + edge
    log2e = np.float32(np.log2(np.e))
    scale2 = np.float32(HEAD_DIM ** -0.5) * log2e
    head = lambda ref, qh: ref[:, qh * HEAD_DIM:(qh + 1) * HEAD_DIM]
    q_all = jnp.stack([jnp.concatenate([head(q_ref, h * q_per_kv + g) for g in range(q_per_kv)], axis=0)
                       for h in range(n_kv)])
    k_all = jnp.stack([jnp.concatenate([head(r, h) for r in (kctx_ref, kp_ref, kc_ref, kn_ref)], axis=0)
                       for h in range(n_kv)])
    v_all = jnp.stack([jnp.concatenate([head(r, h) for r in (vctx_ref, vp_ref, vc_ref, vn_ref)], axis=0)
                       for h in range(n_kv)])
    sink2 = jnp.stack([jnp.concatenate([jnp.full((BLOCK, 1), sink_ref[h * q_per_kv + g] * log2e, F32)
                                        for g in range(q_per_kv)], axis=0) for h in range(n_kv)])
    s = jnp.einsum('hqd,hkd->hqk', q_all, k_all, preferred_element_type=F32) * scale2 + bias[None]
    m = jnp.maximum(jnp.max(s, axis=-1, keepdims=True), sink2)
    p = jnp.exp2(s - m)
    denom = jnp.sum(p, axis=-1, keepdims=True) + jnp.exp2(sink2 - m)
    o = jnp.einsum('hqk,hkd->hqd', p.astype(BF16), v_all, preferred_element_type=F32) * (1.0 / denom)
    ss_attn = jnp.zeros((BLOCK, 1), F32)
    for h in range(n_kv):
        for g in range(q_per_kv):
            qh = h * q_per_kv + g
            og = o[h, g * BLOCK:(g + 1) * BLOCK, :]
            ss_attn = ss_attn + jnp.sum(og * og, axis=-1, keepdims=True)
            ybuf[:, mlp_w + qh * HEAD_DIM:mlp_w + (qh + 1) * HEAD_DIM] = og

    rs_mlp = lax.rsqrt(ss_mlp * np.float32(1.0 / mlp_w) + EPS)
    rs_attn = lax.rsqrt(ss_attn * np.float32(1.0 / attn_w) + EPS)
    o_ref[:, :mlp_w] = (ybuf[:, :mlp_w] * rs_mlp * gmix_ref[:, :mlp_w]).astype(BF16)
    o_ref[:, mlp_w:] = (ybuf[:, mlp_w:] * rs_attn * gmix_ref[:, mlp_w:]).astype(BF16)


def _band_masks(q_per_kv, c_len):
    n_keys = c_len + 3 * BLOCK
    r = np.arange(q_per_kv * BLOCK)[:, None] % BLOCK
    col = np.arange(n_keys)[None, :]
    k_off = col - (c_len + BLOCK)
    in_band = (col < c_len) | (np.abs(k_off - r) <= WINDOW)
    band = np.where(in_band, 0.0, -np.inf).astype(np.float32)
    is_prev = (col >= c_len) & (col < c_len + BLOCK)
    is_next = col >= c_len + 2 * BLOCK
    edge = np.where(np.concatenate([is_prev, is_next], axis=0), -np.inf, 0.0).astype(np.float32)
    return jnp.asarray(band), jnp.asarray(edge)


def _mixer(gg, qk, vv, kvctx, w_s, b_sT, sink, g_mix, batch, seq, n_groups, n_kv, q_per_kv, c_len):
    nb = seq // BLOCK
    mlp_w = n_groups * HEAD_DIM
    attn_w = n_kv * q_per_kv * HEAD_DIM
    kv_w = n_kv * HEAD_DIM
    n_keys = c_len + 3 * BLOCK
    band, edge = _band_masks(q_per_kv, c_len)
    row = lambda b, i: b * nb + i
    prev = lambda b, i: b * nb + jnp.maximum(i - 1, 0)
    nxt = lambda b, i: b * nb + jnp.minimum(i + 1, nb - 1)
    wide = lambda c: pl.BlockSpec((BLOCK, mlp_w), lambda b, i: (row(b, i), c))
    kvs = lambda rfn, c: pl.BlockSpec((BLOCK, kv_w), lambda b, i: (rfn(b, i), c))
    k_col = attn_w // kv_w
    const = lambda shape: pl.BlockSpec(shape, lambda b, i: (0,) * len(shape))
    return pl.pallas_call(
        functools.partial(_mixer_kernel, n_groups=n_groups, n_kv=n_kv, q_per_kv=q_per_kv, seq=seq),
        grid=(batch, nb),
        in_specs=[pl.BlockSpec(memory_space=pltpu.SMEM),
                  wide(0), wide(1), pl.BlockSpec((BLOCK, attn_w), lambda b, i: (row(b, i), 0)),
                  kvs(prev, k_col), kvs(row, k_col), kvs(nxt, k_col),
                  kvs(prev, 0), kvs(row, 0), kvs(nxt, 0),
                  pl.BlockSpec((c_len, kv_w), lambda b, i: (b, 0)),
                  pl.BlockSpec((c_len, kv_w), lambda b, i: (b, 1)),
                  const((n_groups, CHUNK, CHUNK)), const((CHUNK, n_groups)), const((1, mlp_w + attn_w)),
                  const((q_per_kv * BLOCK, n_keys)), const((2, n_keys))],
        out_specs=pl.BlockSpec((BLOCK, mlp_w + attn_w), lambda b, i: (row(b, i), 0)),
        out_shape=jax.ShapeDtypeStruct((batch * seq, mlp_w + attn_w), BF16),
        scratch_shapes=[pltpu.VMEM((BLOCK, mlp_w + attn_w), F32)],
        compiler_params=_cparams(("arbitrary", "arbitrary")),
        name="mixer_core",
    )(sink, gg, gg, qk, qk, qk, qk, vv, vv, vv, kvctx, kvctx, w_s, b_sT, g_mix, band, edge)


def _rope_tables(n):
    rows = n // GRID_W
    row = jnp.broadcast_to(jnp.arange(rows, dtype=F32)[:, None], (rows, GRID_W)).reshape(n)
    col = jnp.broadcast_to(jnp.arange(GRID_W, dtype=F32)[None, :], (rows, GRID_W)).reshape(n)
    n_freq = HEAD_DIM // 4
    inv_freq = ROPE_BASE ** (-jnp.arange(n_freq, dtype=F32) / n_freq)
    ang = jnp.concatenate([row[:, None] * inv_freq, col[:, None] * inv_freq], axis=-1)
    cs, sn = jnp.cos(ang), jnp.sin(ang)
    return jnp.concatenate([cs, cs], axis=-1), jnp.concatenate([-sn, sn], axis=-1)


def kernel(x, c, ctx, c_ctx, w_ada, b_ada, w_ffn_gate, w_ffn_up, w_ffn_down, w_in, w_spatial, b_spatial,
           sink_logit, g_mix, w_out, ln_gain, ln_bias):
    b, n, d = x.shape
    c_len = ctx.shape[1]
    depth = w_ada.shape[0]
    d_ff = w_ffn_gate.shape[-1]
    n_groups = w_spatial.shape[1]
    mlp_w = n_groups * HEAD_DIM
    n_q = sink_logit.shape[1]
    attn_w = n_q * HEAD_DIM
    kv_w = (w_in.shape[-1] - 2 * mlp_w - attn_w) // 2
    n_kv = kv_w // HEAD_DIM
    q_per_kv = n_q // n_kv
    assert depth == 1 and b + 1 <= 8
    alpha = float((2.0 * depth) ** 0.25)
    ctx_seg = b

    x_seg = lambda r0: r0 // n
    c_seg = lambda r0: ctx_seg

    cosf, sinf = _rope_tables(n)
    x2 = x.reshape(b * n, d)
    ctx2 = ctx.reshape(b * c_len, d)

    layer = 0
    c8 = jnp.concatenate([c, c_ctx[None, :], jnp.zeros((8 - b - 1, d), F32)], axis=0)
    m = _ada(c8, w_ada[layer], b_ada[layer]).reshape(8, N_SUB, N_MOD, d)
    mod = lambda s, k: m[:, s, k, :][:, None, :]

    def ffn(h, hm, idx, s, gain, bias, seg_fn, next_mod, tm_up):
        a = _swiglu_up(hm, w_ffn_gate, w_ffn_up, (layer, idx), tm=tm_up, tn=FF_TILE)
        y = _matmul(a, w_ffn_down, (layer, idx), F32, tm=1024, tn=512, tk=d_ff // 2, name="ffn_down")
        return _resid_ln(h, y, mod(s, 2), gain, bias, seg_fn, alpha, 0.5, next_mod)

    xm = _modcast(x2, mod(0, 0), mod(0, 1), x_seg)
    cm = _modcast(ctx2, mod(0, 0), mod(0, 1), c_seg)
    mix_mod = (mod(1, 0), mod(1, 1))
    x1, hx = ffn(x2, xm, 0, 0, ln_gain[layer, 0], ln_bias[layer, 0], x_seg, mix_mod, UP_TM)
    _, hc = ffn(ctx2, cm, 0, 0, ln_gain[layer, 0], ln_bias[layer, 0], c_seg, mix_mod, 1024)

    gg = _proj(hx, w_in, (layer,), 0, 2 * mlp_w)
    qk = _proj(hx, w_in, (layer,), 2 * mlp_w, attn_w + kv_w, rope=(cosf, sinf, n))
    vv = _matmul(hx, w_in, (layer,), BF16, tm=1024, tn=kv_w, tk=d, name="in_proj_v",
                 col0=2 * mlp_w + attn_w + kv_w, n=kv_w)
    kvctx = _matmul(hc, w_in, (layer,), BF16, tm=b * c_len, tn=kv_w, tk=d, name="ctx_kv",
                    col0=2 * mlp_w + attn_w)
    ymix = _mixer(gg, qk, vv, kvctx, w_spatial[layer].astype(BF16), b_spatial[layer].T, sink_logit[layer],
                  g_mix[layer].reshape(1, mlp_w + attn_w), b, n, n_groups, n_kv, q_per_kv, c_len)
    mix = _matmul(ymix, w_out, (layer,), F32, tm=1024, tn=512, tk=mlp_w + attn_w, name="out_proj")
    x_mid, hm2 = _resid_ln(x1, mix, mod(1, 2), ln_gain[layer, 1], ln_bias[layer, 1], x_seg, alpha, 1.0,
                           (mod(2, 0), mod(2, 1)))

    out, _ = ffn(x_mid, hm2, 1, 2, ln_gain[layer, 2], ln_bias[layer, 2], x_seg, None, UP_TM)
    return out.reshape(b, n, d)
```

```python
import functools

import jax
import jax.numpy as jnp
import numpy as np
from jax import lax
from jax.experimental import pallas as pl
from jax.experimental.pallas import tpu as pltpu

F32 = jnp.float32
BF16 = jnp.bfloat16

HEAD_DIM = 128
CHUNK = 128
WINDOW = 128
BLOCK = 128
GRID_W = 64
ROPE_BASE = 10000.0
EPS = 1e-6
N_SUB = 3
N_MOD = 3

VMEM_LIMIT_BYTES = 58 * 1024 * 1024
FF_TILE = 256
UP_TM = 2048


def _cparams(sem):
    return pltpu.CompilerParams(dimension_semantics=sem, vmem_limit_bytes=VMEM_LIMIT_BYTES)


def _ada_kernel(c_ref, w_ref, b_ref, o_ref):
    c = c_ref[...]
    s = (c * jax.nn.sigmoid(c)).astype(BF16)
    o_ref[...] = jnp.dot(s, w_ref[...].astype(BF16), preferred_element_type=F32) + b_ref[...]


def _ada(c8, w_ada, b_ada, tn=512):
    d, n = w_ada.shape
    return pl.pallas_call(
        _ada_kernel,
        grid=(n // tn,),
        in_specs=[pl.BlockSpec((8, d), lambda j: (0, 0)),
                  pl.BlockSpec((d, tn), lambda j: (0, j)),
                  pl.BlockSpec((1, tn), lambda j: (0, j))],
        out_specs=pl.BlockSpec((8, tn), lambda j: (0, j)),
        out_shape=jax.ShapeDtypeStruct((8, n), F32),
        compiler_params=_cparams(("arbitrary",)),
        name="ada",
    )(c8, w_ada, b_ada.reshape(1, n))


def _modcast_kernel(x_ref, shift_ref, scale_ref, o_ref):
    o_ref[...] = (x_ref[...] * (1.0 + scale_ref[...]) + shift_ref[...]).astype(BF16)


def _modcast(x2d, shift, scale, seg_fn, tm=256):
    r, d = x2d.shape
    vec = pl.BlockSpec((None, 1, d), lambda i: (seg_fn(i * tm), 0, 0))
    return pl.pallas_call(
        _modcast_kernel,
        grid=(r // tm,),
        in_specs=[pl.BlockSpec((tm, d), lambda i: (i, 0)), vec, vec],
        out_specs=pl.BlockSpec((tm, d), lambda i: (i, 0)),
        out_shape=jax.ShapeDtypeStruct((r, d), BF16),
        compiler_params=_cparams(("arbitrary",)),
        name="modcast",
    )(x2d, shift, scale)


def _swiglu_up_kernel(x_ref, wg_ref, wu_ref, o_ref):
    x = x_ref[...]
    g = jnp.dot(x, wg_ref[...].astype(BF16), preferred_element_type=F32)
    u = jnp.dot(x, wu_ref[...].astype(BF16), preferred_element_type=F32)
    o_ref[...] = (g * jax.nn.sigmoid(g) * u).astype(BF16)


def _swiglu_up(xm, w_gate, w_up, lead, tm, tn):
    r, d = xm.shape
    f = w_gate.shape[-1]
    wspec = pl.BlockSpec((None,) * len(lead) + (d, tn), lambda i, j: lead + (0, j))
    return pl.pallas_call(
        _swiglu_up_kernel,
        grid=(r // tm, f // tn),
        in_specs=[pl.BlockSpec((tm, d), lambda i, j: (i, 0)), wspec, wspec],
        out_specs=pl.BlockSpec((tm, tn), lambda i, j: (i, j)),
        out_shape=jax.ShapeDtypeStruct((r, f), BF16),
        compiler_params=_cparams(("arbitrary", "arbitrary")),
        name="swiglu_up",
    )(xm, w_gate, w_up)


def _mm_kernel(a_ref, b_ref, o_ref, *, nk):
    d = jnp.dot(a_ref[...], b_ref[...].astype(BF16), preferred_element_type=F32)
    if nk == 1:
        o_ref[...] = d.astype(o_ref.dtype)
    else:
        k = pl.program_id(2)

        @pl.when(k == 0)
        def _():
            o_ref[...] = d

        @pl.when(k > 0)
        def _():
            o_ref[...] += d


def _matmul(a, b, lead, out_dtype, tm, tn, tk, name, col0=0, n=None):
    r, kk = a.shape
    n = b.shape[-1] - col0 if n is None else n
    nk = kk // tk
    assert nk * tk == kk and col0 % tn == 0 and n % tn == 0
    assert nk == 1 or out_dtype == F32
    return pl.pallas_call(
        functools.partial(_mm_kernel, nk=nk),
        grid=(r // tm, n // tn, nk),
        in_specs=[pl.BlockSpec((tm, tk), lambda i, j, k: (i, k)),
                  pl.BlockSpec((None,) * len(lead) + (tk, tn), lambda i, j, k: lead + (k, j + col0 // tn))],
        out_specs=pl.BlockSpec((tm, tn), lambda i, j, k: (i, j)),
        out_shape=jax.ShapeDtypeStruct((r, n), out_dtype),
        compiler_params=_cparams(("arbitrary", "arbitrary", "arbitrary")),
        name=name,
    )(a, b)


def _mm_resid_kernel(*refs, nk, alpha, coef, normed):
    if normed:
        a_ref, b_ref, res_ref, gate_ref, mu_ref, rstd_ref, gain_ref, bias_ref, o_ref = refs
    else:
        a_ref, b_ref, res_ref, gate_ref, o_ref = refs
    def partial_product():
        return jnp.dot(a_ref[...], b_ref[...].astype(BF16), preferred_element_type=F32)

    def finish(y):
        h = res_ref[...]
        if normed:
            h = (h - mu_ref[...]) * rstd_ref[...] * gain_ref[...] + bias_ref[...]
        o_ref[...] = alpha * h + (coef * gate_ref[...]) * y

    if nk == 1:
        finish(partial_product())
    else:
        k = pl.program_id(2)

        @pl.when(k == 0)
        def _():
            o_ref[...] = partial_product()

        if nk > 2:
            @pl.when((k > 0) & (k < nk - 1))
            def _():
                o_ref[...] += partial_product()

        @pl.when(k == nk - 1)
        def _():
            finish(o_ref[...] + partial_product())


def _matmul_resid(a, b, lead, res, gate, seg_fn, alpha, coef, tm, tn, tk, name, norm=None):
    r, kk = a.shape
    n = b.shape[-1]
    nk = kk // tk
    assert nk * tk == kk and nk >= 1 and n % tn == 0
    tile = pl.BlockSpec((tm, tn), lambda i, j, k: (i, j))
    vec = pl.BlockSpec((None, 1, tn), lambda i, j, k: (seg_fn(i * tm), 0, j))
    in_specs = [pl.BlockSpec((tm, tk), lambda i, j, k: (i, k)),
                pl.BlockSpec((None,) * len(lead) + (tk, tn), lambda i, j, k: lead + (k, j)),
                tile, vec]
    args = [a, b, res, gate]
    if norm is not None:
        mu, rstd, gain, bias = norm
        stat = pl.BlockSpec((tm, 1), lambda i, j, k: (i, 0))
        par = pl.BlockSpec((1, tn), lambda i, j, k: (0, j))
        in_specs += [stat, stat, par, par]
        args += [mu, rstd, gain.reshape(1, n), bias.reshape(1, n)]
    return pl.pallas_call(
        functools.partial(_mm_resid_kernel, nk=nk, alpha=alpha, coef=coef, normed=norm is not None),
        grid=(r // tm, n // tn, nk),
        in_specs=in_specs,
        out_specs=tile,
        out_shape=jax.ShapeDtypeStruct((r, n), F32),
        compiler_params=_cparams(("arbitrary", "arbitrary", "arbitrary")),
        name=name,
    )(*args)


def _ln_kernel(*refs, final):
    if final:
        z_ref, gain_ref, bias_ref, o_ref = refs
    else:
        z_ref, gain_ref, bias_ref, shift_ref, scale_ref, mu_ref, rstd_ref, om_ref = refs
    z = z_ref[...]
    mu = jnp.mean(z, axis=-1, keepdims=True)
    zc = z - mu
    var = jnp.mean(zc * zc, axis=-1, keepdims=True)
    rstd = lax.rsqrt(var + EPS)
    hn = zc * rstd * gain_ref[...] + bias_ref[...]
    if final:
        o_ref[...] = hn
    else:
        mu_ref[...] = mu
        rstd_ref[...] = rstd
        om_ref[...] = (hn * (1.0 + scale_ref[...]) + shift_ref[...]).astype(BF16)


def _ln(z, gain, bias, seg_fn=None, next_mod=None, tm=256):
    r, d = z.shape
    row = pl.BlockSpec((tm, d), lambda i: (i, 0))
    full = pl.BlockSpec((1, d), lambda i: (0, 0))
    final = next_mod is None
    in_specs = [row, full, full]
    args = [z, gain.reshape(1, d), bias.reshape(1, d)]
    if final:
        out_specs = row
        out_shape = jax.ShapeDtypeStruct((r, d), F32)
    else:
        vec = pl.BlockSpec((None, 1, d), lambda i: (seg_fn(i * tm), 0, 0))
        stat = pl.BlockSpec((tm, 1), lambda i: (i, 0))
        in_specs += [vec, vec]
        args += list(next_mod)
        out_specs = [stat, stat, row]
        out_shape = [jax.ShapeDtypeStruct((r, 1), F32), jax.ShapeDtypeStruct((r, 1), F32),
                     jax.ShapeDtypeStruct((r, d), BF16)]
    return pl.pallas_call(
        functools.partial(_ln_kernel, final=final),
        grid=(r // tm,),
        in_specs=in_specs,
        out_specs=out_specs,
        out_shape=out_shape,
        compiler_params=_cparams(("arbitrary",)),
        name="ln_final" if final else "ln_stats",
    )(*args)


def _proj_gelu_kernel(x_ref, w_ref, o_ref):
    acc = jnp.dot(x_ref[...], w_ref[...].astype(BF16), preferred_element_type=F32)
    o_ref[...] = (0.5 * acc * (1.0 + lax.erf(acc * np.float32(np.sqrt(0.5))))).astype(BF16)


def _proj_rope_kernel(x_ref, w_ref, cos_ref, sin_ref, o_ref, *, heads_per_tile):
    acc = jnp.dot(x_ref[...], w_ref[...].astype(BF16), preferred_element_type=F32)
    cs = cos_ref[...]
    sn = sin_ref[...]
    for hh in range(heads_per_tile):
        t = acc[:, hh * HEAD_DIM:(hh + 1) * HEAD_DIM]
        rot = pltpu.roll(t, HEAD_DIM // 2, 1)
        o_ref[:, hh * HEAD_DIM:(hh + 1) * HEAD_DIM] = (t * cs + rot * sn).astype(BF16)


def _proj(hx, w_in, lead, col0, n, rope=None, tm=2048, tn=256):
    r, d = hx.shape
    assert col0 % tn == 0 and n % tn == 0
    in_specs = [pl.BlockSpec((tm, d), lambda i, j: (i, 0)),
                pl.BlockSpec((None,) * len(lead) + (d, tn), lambda i, j: lead + (0, j + col0 // tn))]
    args = [hx, w_in]
    if rope is None:
        body, name = _proj_gelu_kernel, "in_proj_gelu"
    else:
        cosf, sinf, seq = rope
        tiles_per_seq = seq // tm
        tab = pl.BlockSpec((tm, HEAD_DIM), lambda i, j: (i % tiles_per_seq, 0))
        in_specs += [tab, tab]
        args += [cosf, sinf]
        body, name = functools.partial(_proj_rope_kernel, heads_per_tile=tn // HEAD_DIM), "in_proj_rope"
    return pl.pallas_call(
        body,
        grid=(r // tm, n // tn),
        in_specs=in_specs,
        out_specs=pl.BlockSpec((tm, tn), lambda i, j: (i, j)),
        out_shape=jax.ShapeDtypeStruct((r, n), BF16),
        compiler_params=_cparams(("arbitrary", "arbitrary")),
        name=name,
    )(*args)


def _mixer_kernel(sink_ref, gu_ref, gv_ref, q_ref, kp_ref, kc_ref, kn_ref, vp_ref, vc_ref, vn_ref,
                  kctx_ref, vctx_ref, ws_ref, bs_ref, gmix_ref, band_ref, edge_ref, o_ref, ybuf,
                  *, n_groups, n_kv, q_per_kv, seq):
    blk = pl.program_id(1)
    mlp_w = n_groups * HEAD_DIM
    attn_w = n_kv * q_per_kv * HEAD_DIM

    gv = gv_ref[...].astype(F32)
    mu = jnp.mean(gv, axis=-1, keepdims=True)
    vc_ = gv - mu
    var = jnp.mean(vc_ * vc_, axis=-1, keepdims=True)
    vstd = (vc_ * lax.rsqrt(var + EPS)).astype(BF16)
    ss_mlp = jnp.zeros((CHUNK, 1), F32)
    for g in range(n_groups):
        sl = slice(g * HEAD_DIM, (g + 1) * HEAD_DIM)
        mixed = jnp.dot(ws_ref[g], vstd[:, sl], preferred_element_type=F32) + bs_ref[:, g:g + 1]
        ym = gu_ref[:, sl].astype(F32) * mixed
        ss_mlp = ss_mlp + jnp.sum(ym * ym, axis=-1, keepdims=True)
        ybuf[:, sl] = ym

    edge = (jnp.where(blk == 0, edge_ref[0:1, :], 0.0)
            + jnp.where(blk == seq // BLOCK - 1, edge_ref[1:2, :], 0.0))
    bias = band_ref[...] + edge
    log2e = np.float32(np.log2(np.e))
    scale2 = np.float32(HEAD_DIM ** -0.5) * log2e
    head = lambda ref, qh: ref[:, qh * HEAD_DIM:(qh + 1) * HEAD_DIM]
    q_all = jnp.stack([jnp.concatenate([head(q_ref, h * q_per_kv + g) for g in range(q_per_kv)], axis=0)
                       for h in range(n_kv)])
    k_all = jnp.stack([jnp.concatenate([head(r, h) for r in (kctx_ref, kp_ref, kc_ref, kn_ref)], axis=0)
                       for h in range(n_kv)])
    v_all = jnp.stack([jnp.concatenate([head(r, h) for r in (vctx_ref, vp_ref, vc_ref, vn_ref)], axis=0)
                       for h in range(n_kv)])
    sink2 = jnp.stack([jnp.concatenate([jnp.full((BLOCK, 1), sink_ref[h * q_per_kv + g] * log2e, F32)
                                        for g in range(q_per_kv)], axis=0) for h in range(n_kv)])
    s = jnp.einsum('hqd,hkd->hqk', q_all, k_all, preferred_element_type=F32) * scale2 + bias[None]
    m = jnp.maximum(jnp.max(s, axis=-1, keepdims=True), sink2)
    p = jnp.exp2(s - m)
    denom = jnp.sum(p, axis=-1, keepdims=True) + jnp.exp2(sink2 - m)
    o = jnp.einsum('hqk,hkd->hqd', p.astype(BF16), v_all, preferred_element_type=F32) * (1.0 / denom)
    ss_attn = jnp.zeros((BLOCK, 1), F32)
    for h in range(n_kv):
        for g in range(q_per_kv):
            qh = h * q_per_kv + g
            og = o[h, g * BLOCK:(g + 1) * BLOCK, :]
            ss_attn = ss_attn + jnp.sum(og * og, axis=-1, keepdims=True)
            ybuf[:, mlp_w + qh * HEAD_DIM:mlp_w + (qh + 1) * HEAD_DIM] = og

    rs_mlp = lax.rsqrt(ss_mlp * np.float32(1.0 / mlp_w) + EPS)
    rs_attn = lax.rsqrt(ss_attn * np.float32(1.0 / attn_w) + EPS)
    o_ref[:, :mlp_w] = (ybuf[:, :mlp_w] * rs_mlp * gmix_ref[:, :mlp_w]).astype(BF16)
    o_ref[:, mlp_w:] = (ybuf[:, mlp_w:] * rs_attn * gmix_ref[:, mlp_w:]).astype(BF16)


def _band_masks(q_per_kv, c_len):
    n_keys = c_len + 3 * BLOCK
    r = np.arange(q_per_kv * BLOCK)[:, None] % BLOCK
    col = np.arange(n_keys)[None, :]
    k_off = col - (c_len + BLOCK)
    in_band = (col < c_len) | (np.abs(k_off - r) <= WINDOW)
    band = np.where(in_band, 0.0, -np.inf).astype(np.float32)
    is_prev = (col >= c_len) & (col < c_len + BLOCK)
    is_next = col >= c_len + 2 * BLOCK
    edge = np.where(np.concatenate([is_prev, is_next], axis=0), -np.inf, 0.0).astype(np.float32)
    return jnp.asarray(band), jnp.asarray(edge)


def _mixer(gg, qk, vv, kvctx, w_s, b_sT, sink, g_mix, batch, seq, n_groups, n_kv, q_per_kv, c_len):
    nb = seq // BLOCK
    mlp_w = n_groups * HEAD_DIM
    attn_w = n_kv * q_per_kv * HEAD_DIM
    kv_w = n_kv * HEAD_DIM
    n_keys = c_len + 3 * BLOCK
    band, edge = _band_masks(q_per_kv, c_len)
    row = lambda b, i: b * nb + i
    prev = lambda b, i: b * nb + jnp.maximum(i - 1, 0)
    nxt = lambda b, i: b * nb + jnp.minimum(i + 1, nb - 1)
    wide = lambda c: pl.BlockSpec((BLOCK, mlp_w), lambda b, i: (row(b, i), c))
    kvs = lambda rfn, c: pl.BlockSpec((BLOCK, kv_w), lambda b, i: (rfn(b, i), c))
    k_col = attn_w // kv_w
    const = lambda shape: pl.BlockSpec(shape, lambda b, i: (0,) * len(shape))
    return pl.pallas_call(
        functools.partial(_mixer_kernel, n_groups=n_groups, n_kv=n_kv, q_per_kv=q_per_kv, seq=seq),
        grid=(batch, nb),
        in_specs=[pl.BlockSpec(memory_space=pltpu.SMEM),
                  wide(0), wide(1), pl.BlockSpec((BLOCK, attn_w), lambda b, i: (row(b, i), 0)),
                  kvs(prev, k_col), kvs(row, k_col), kvs(nxt, k_col),
                  kvs(prev, 0), kvs(row, 0), kvs(nxt, 0),
                  pl.BlockSpec((c_len, kv_w), lambda b, i: (b, 0)),
                  pl.BlockSpec((c_len, kv_w), lambda b, i: (b, 1)),
                  const((n_groups, CHUNK, CHUNK)), const((CHUNK, n_groups)), const((1, mlp_w + attn_w)),
                  const((q_per_kv * BLOCK, n_keys)), const((2, n_keys))],
        out_specs=pl.BlockSpec((BLOCK, mlp_w + attn_w), lambda b, i: (row(b, i), 0)),
        out_shape=jax.ShapeDtypeStruct((batch * seq, mlp_w + attn_w), BF16),
        scratch_shapes=[pltpu.VMEM((BLOCK, mlp_w + attn_w), F32)],
        compiler_params=_cparams(("arbitrary", "arbitrary")),
        name="mixer_core",
    )(sink, gg, gg, qk, qk, qk, qk, vv, vv, vv, kvctx, kvctx, w_s, b_sT, g_mix, band, edge)


def _rope_tables(n):
    rows = n // GRID_W
    row = jnp.broadcast_to(jnp.arange(rows, dtype=F32)[:, None], (rows, GRID_W)).reshape(n)
    col = jnp.broadcast_to(jnp.arange(GRID_W, dtype=F32)[None, :], (rows, GRID_W)).reshape(n)
    n_freq = HEAD_DIM // 4
    inv_freq = ROPE_BASE ** (-jnp.arange(n_freq, dtype=F32) / n_freq)
    ang = jnp.concatenate([row[:, None] * inv_freq, col[:, None] * inv_freq], axis=-1)
    cs, sn = jnp.cos(ang), jnp.sin(ang)
    return jnp.concatenate([cs, cs], axis=-1), jnp.concatenate([-sn, sn], axis=-1)


def kernel(x, c, ctx, c_ctx, w_ada, b_ada, w_ffn_gate, w_ffn_up, w_ffn_down, w_in, w_spatial, b_spatial,
           sink_logit, g_mix, w_out, ln_gain, ln_bias):
    b, n, d = x.shape
    c_len = ctx.shape[1]
    depth = w_ada.shape[0]
    d_ff = w_ffn_gate.shape[-1]
    n_groups = w_spatial.shape[1]
    mlp_w = n_groups * HEAD_DIM
    n_q = sink_logit.shape[1]
    attn_w = n_q * HEAD_DIM
    kv_w = (w_in.shape[-1] - 2 * mlp_w - attn_w) // 2
    n_kv = kv_w // HEAD_DIM
    q_per_kv = n_q // n_kv
    assert depth == 1 and b + 1 <= 8
    alpha = float((2.0 * depth) ** 0.25)
    ctx_seg = b

    x_seg = lambda r0: r0 // n
    c_seg = lambda r0: ctx_seg

    cosf, sinf = _rope_tables(n)
    x2 = x.reshape(b * n, d)
    ctx2 = ctx.reshape(b * c_len, d)

    layer = 0
    c8 = jnp.concatenate([c, c_ctx[None, :], jnp.zeros((8 - b - 1, d), F32)], axis=0)
    m = _ada(c8, w_ada[layer], b_ada[layer]).reshape(8, N_SUB, N_MOD, d)
    mod = lambda s, k: m[:, s, k, :][:, None, :]

    w_down_b = w_ffn_down.astype(BF16)

    def ffn_z(res, hm, idx, s, seg_fn, tm_up, norm=None):
        a = _swiglu_up(hm, w_ffn_gate, w_ffn_up, (layer, idx), tm=tm_up, tn=FF_TILE)
        return _matmul_resid(a, w_down_b, (layer, idx), res, mod(s, 2), seg_fn, alpha, 0.5,
                             tm=1024, tn=512, tk=d_ff // 2, name="ffn_down", norm=norm)

    xm = _modcast(x2, mod(0, 0), mod(0, 1), x_seg)
    cm = _modcast(ctx2, mod(0, 0), mod(0, 1), c_seg)
    mix_mod = (mod(1, 0), mod(1, 1))
    ln0 = (ln_gain[layer, 0], ln_bias[layer, 0])
    z1 = ffn_z(x2, xm, 0, 0, x_seg, UP_TM)
    z1c = ffn_z(ctx2, cm, 0, 0, c_seg, 1024)
    mu1, rstd1, hx = _ln(z1, *ln0, seg_fn=x_seg, next_mod=mix_mod)
    _, _, hc = _ln(z1c, *ln0, seg_fn=c_seg, next_mod=mix_mod)

    gg = _proj(hx, w_in, (layer,), 0, 2 * mlp_w)
    qk = _proj(hx, w_in, (layer,), 2 * mlp_w, attn_w + kv_w, rope=(cosf, sinf, n))
    vv = _matmul(hx, w_in, (layer,), BF16, tm=2048, tn=256, tk=d, name="in_proj_v",
                 col0=2 * mlp_w + attn_w + kv_w, n=kv_w)
    kvctx = _matmul(hc, w_in, (layer,), BF16, tm=b * c_len, tn=kv_w, tk=d, name="ctx_kv",
                    col0=2 * mlp_w + attn_w)
    ymix = _mixer(gg, qk, vv, kvctx, w_spatial[layer].astype(BF16), b_spatial[layer].T, sink_logit[layer],
                  g_mix[layer].reshape(1, mlp_w + attn_w), b, n, n_groups, n_kv, q_per_kv, c_len)
    z2 = _matmul_resid(ymix, w_out.astype(BF16), (layer,), z1, mod(1, 2), x_seg, alpha, 1.0, tm=1024, tn=512,
                       tk=mlp_w + attn_w, name="out_proj", norm=(mu1, rstd1) + ln0)
    ln1 = (ln_gain[layer, 1], ln_bias[layer, 1])
    mu2, rstd2, hm2 = _ln(z2, *ln1, seg_fn=x_seg, next_mod=(mod(2, 0), mod(2, 1)))

    z3 = ffn_z(z2, hm2, 1, 2, x_seg, UP_TM, norm=(mu2, rstd2) + ln1)
    out = _ln(z3, ln_gain[layer, 2], ln_bias[layer, 2])
    return out.reshape(b, n, d)
```

```python
import functools

import jax
import jax.numpy as jnp
import numpy as np
from jax import lax
from jax.experimental import pallas as pl
from jax.experimental.pallas import tpu as pltpu

F32 = jnp.float32
BF16 = jnp.bfloat16

HEAD_DIM = 128
CHUNK = 128
WINDOW = 128
BLOCK = 128
GRID_W = 64
ROPE_BASE = 10000.0
EPS = 1e-6
N_SUB = 3
N_MOD = 3

VMEM_LIMIT_BYTES = 58 * 1024 * 1024
MIXER_VMEM_LIMIT_BYTES = 62 * 1024 * 1024
FF_TILE = 256
UP_TM = 2048
ATTN_HEAD_GROUP = 2
OUT_PROJ_CHUNKS = 8
LN_ROW_CHUNK = 16


def _cparams(sem):
    return pltpu.CompilerParams(dimension_semantics=sem, vmem_limit_bytes=VMEM_LIMIT_BYTES)


def _ada_kernel(c_ref, w_ref, b_ref, o_ref):
    c = c_ref[...]
    s = (c * jax.nn.sigmoid(c)).astype(BF16)
    o_ref[...] = jnp.dot(s, w_ref[...].astype(BF16), preferred_element_type=F32) + b_ref[...]


def _ada(c8, w_ada, b_ada, tn=512):
    d, n = w_ada.shape
    return pl.pallas_call(
        _ada_kernel,
        grid=(n // tn,),
        in_specs=[pl.BlockSpec((8, d), lambda j: (0, 0)),
                  pl.BlockSpec((d, tn), lambda j: (0, j)),
                  pl.BlockSpec((1, tn), lambda j: (0, j))],
        out_specs=pl.BlockSpec((8, tn), lambda j: (0, j)),
        out_shape=jax.ShapeDtypeStruct((8, n), F32),
        compiler_params=_cparams(("arbitrary",)),
        name="ada",
    )(c8, w_ada, b_ada.reshape(1, n))


def _modcast_kernel(x_ref, shift_ref, scale_ref, o_ref):
    o_ref[...] = (x_ref[...] * (1.0 + scale_ref[...]) + shift_ref[...]).astype(BF16)


def _modcast(x2d, shift, scale, seg_fn, tm=256):
    r, d = x2d.shape
    vec = pl.BlockSpec((None, 1, d), lambda i: (seg_fn(i * tm), 0, 0))
    return pl.pallas_call(
        _modcast_kernel,
        grid=(r // tm,),
        in_specs=[pl.BlockSpec((tm, d), lambda i: (i, 0)), vec, vec],
        out_specs=pl.BlockSpec((tm, d), lambda i: (i, 0)),
        out_shape=jax.ShapeDtypeStruct((r, d), BF16),
        compiler_params=_cparams(("arbitrary",)),
        name="modcast",
    )(x2d, shift, scale)


def _swiglu_up_kernel(x_ref, wg_ref, wu_ref, o_ref):
    x = x_ref[...]
    g = jnp.dot(x, wg_ref[...].astype(BF16), preferred_element_type=F32)
    u = jnp.dot(x, wu_ref[...].astype(BF16), preferred_element_type=F32)
    o_ref[...] = (g * jax.nn.sigmoid(g) * u).astype(BF16)


def _swiglu_up(xm, w_gate, w_up, lead, tm, tn):
    r, d = xm.shape
    f = w_gate.shape[-1]
    wspec = pl.BlockSpec((None,) * len(lead) + (d, tn), lambda i, j: lead + (0, j))
    return pl.pallas_call(
        _swiglu_up_kernel,
        grid=(r // tm, f // tn),
        in_specs=[pl.BlockSpec((tm, d), lambda i, j: (i, 0)), wspec, wspec],
        out_specs=pl.BlockSpec((tm, tn), lambda i, j: (i, j)),
        out_shape=jax.ShapeDtypeStruct((r, f), BF16),
        compiler_params=_cparams(("arbitrary", "arbitrary")),
        name="swiglu_up",
    )(xm, w_gate, w_up)


def _mm_kernel(a_ref, b_ref, o_ref, *, nk):
    d = jnp.dot(a_ref[...], b_ref[...].astype(BF16), preferred_element_type=F32)
    if nk == 1:
        o_ref[...] = d.astype(o_ref.dtype)
    else:
        k = pl.program_id(2)

        @pl.when(k == 0)
        def _():
            o_ref[...] = d

        @pl.when(k > 0)
        def _():
            o_ref[...] += d


def _matmul(a, b, lead, out_dtype, tm, tn, tk, name, col0=0, n=None):
    r, kk = a.shape
    n = b.shape[-1] - col0 if n is None else n
    nk = kk // tk
    assert nk * tk == kk and col0 % tn == 0 and n % tn == 0
    assert nk == 1 or out_dtype == F32
    return pl.pallas_call(
        functools.partial(_mm_kernel, nk=nk),
        grid=(r // tm, n // tn, nk),
        in_specs=[pl.BlockSpec((tm, tk), lambda i, j, k: (i, k)),
                  pl.BlockSpec((None,) * len(lead) + (tk, tn), lambda i, j, k: lead + (k, j + col0 // tn))],
        out_specs=pl.BlockSpec((tm, tn), lambda i, j, k: (i, j)),
        out_shape=jax.ShapeDtypeStruct((r, n), out_dtype),
        compiler_params=_cparams(("arbitrary", "arbitrary", "arbitrary")),
        name=name,
    )(a, b)


def _mm_resid_kernel(*refs, nk, alpha, coef, normed):
    if normed:
        a_ref, b_ref, res_ref, gate_ref, mu_ref, rstd_ref, gain_ref, bias_ref, o_ref = refs
    else:
        a_ref, b_ref, res_ref, gate_ref, o_ref = refs
    def partial_product():
        return jnp.dot(a_ref[...], b_ref[...].astype(BF16), preferred_element_type=F32)

    def finish(y):
        h = res_ref[...]
        if normed:
            h = (h - mu_ref[...]) * rstd_ref[...] * gain_ref[...] + bias_ref[...]
        o_ref[...] = alpha * h + (coef * gate_ref[...]) * y

    if nk == 1:
        finish(partial_product())
    else:
        k = pl.program_id(2)

        @pl.when(k == 0)
        def _():
            o_ref[...] = partial_product()

        if nk > 2:
            @pl.when((k > 0) & (k < nk - 1))
            def _():
                o_ref[...] += partial_product()

        @pl.when(k == nk - 1)
        def _():
            finish(o_ref[...] + partial_product())


def _matmul_resid(a, b, lead, res, gate, seg_fn, alpha, coef, tm, tn, tk, name, norm=None):
    r, kk = a.shape
    n = b.shape[-1]
    nk = kk // tk
    assert nk * tk == kk and nk >= 1 and n % tn == 0
    tile = pl.BlockSpec((tm, tn), lambda i, j, k: (i, j))
    vec = pl.BlockSpec((None, 1, tn), lambda i, j, k: (seg_fn(i * tm), 0, j))
    in_specs = [pl.BlockSpec((tm, tk), lambda i, j, k: (i, k)),
                pl.BlockSpec((None,) * len(lead) + (tk, tn), lambda i, j, k: lead + (k, j)),
                tile, vec]
    args = [a, b, res, gate]
    if norm is not None:
        mu, rstd, gain, bias = norm
        stat = pl.BlockSpec((tm, 1), lambda i, j, k: (i, 0))
        par = pl.BlockSpec((1, tn), lambda i, j, k: (0, j))
        in_specs += [stat, stat, par, par]
        args += [mu, rstd, gain.reshape(1, n), bias.reshape(1, n)]
    return pl.pallas_call(
        functools.partial(_mm_resid_kernel, nk=nk, alpha=alpha, coef=coef, normed=norm is not None),
        grid=(r // tm, n // tn, nk),
        in_specs=in_specs,
        out_specs=tile,
        out_shape=jax.ShapeDtypeStruct((r, n), F32),
        compiler_params=_cparams(("arbitrary", "arbitrary", "arbitrary")),
        name=name,
    )(*args)


def _ln_kernel(*refs, final):
    if final:
        z_ref, gain_ref, bias_ref, o_ref = refs
    else:
        z_ref, gain_ref, bias_ref, shift_ref, scale_ref, mu_ref, rstd_ref, om_ref = refs
    if final:
        mul, add = gain_ref[...], bias_ref[...]
    else:
        mul = gain_ref[...] * (1.0 + scale_ref[...])
        add = bias_ref[...] * (1.0 + scale_ref[...]) + shift_ref[...]
    for r0 in range(0, z_ref.shape[0], LN_ROW_CHUNK):
        rows = slice(r0, r0 + LN_ROW_CHUNK)
        z = z_ref[rows, :]
        mu = jnp.mean(z, axis=-1, keepdims=True)
        zc = z - mu
        var = jnp.mean(zc * zc, axis=-1, keepdims=True)
        rstd = lax.rsqrt(var + EPS)
        y = zc * rstd * mul + add
        if final:
            o_ref[rows, :] = y
        else:
            mu_ref[rows, :] = mu
            rstd_ref[rows, :] = rstd
            om_ref[rows, :] = y.astype(BF16)


def _ln(z, gain, bias, seg_fn=None, next_mod=None, tm=256):
    r, d = z.shape
    row = pl.BlockSpec((tm, d), lambda i: (i, 0))
    full = pl.BlockSpec((1, d), lambda i: (0, 0))
    final = next_mod is None
    in_specs = [row, full, full]
    args = [z, gain.reshape(1, d), bias.reshape(1, d)]
    if final:
        out_specs = row
        out_shape = jax.ShapeDtypeStruct((r, d), F32)
    else:
        vec = pl.BlockSpec((None, 1, d), lambda i: (seg_fn(i * tm), 0, 0))
        stat = pl.BlockSpec((tm, 1), lambda i: (i, 0))
        in_specs += [vec, vec]
        args += list(next_mod)
        out_specs = [stat, stat, row]
        out_shape = [jax.ShapeDtypeStruct((r, 1), F32), jax.ShapeDtypeStruct((r, 1), F32),
                     jax.ShapeDtypeStruct((r, d), BF16)]
    return pl.pallas_call(
        functools.partial(_ln_kernel, final=final),
        grid=(r // tm,),
        in_specs=in_specs,
        out_specs=out_specs,
        out_shape=out_shape,
        compiler_params=_cparams(("arbitrary",)),
        name="ln_final" if final else "ln_stats",
    )(*args)


def _proj_gelu_kernel(x_ref, w_ref, o_ref):
    acc = jnp.dot(x_ref[...], w_ref[...].astype(BF16), preferred_element_type=F32)
    o_ref[...] = (0.5 * acc * (1.0 + lax.erf(acc * np.float32(np.sqrt(0.5))))).astype(BF16)


def _proj_rope_kernel(x_ref, w_ref, cos_ref, sin_ref, o_ref, *, heads_per_tile):
    acc = jnp.dot(x_ref[...], w_ref[...].astype(BF16), preferred_element_type=F32)
    cs = cos_ref[...]
    sn = sin_ref[...]
    for hh in range(heads_per_tile):
        t = acc[:, hh * HEAD_DIM:(hh + 1) * HEAD_DIM]
        rot = pltpu.roll(t, HEAD_DIM // 2, 1)
        o_ref[:, hh * HEAD_DIM:(hh + 1) * HEAD_DIM] = (t * cs + rot * sn).astype(BF16)


def _proj(hx, w_in, lead, col0, n, rope=None, tm=1024, tn=512):
    r, d = hx.shape
    assert col0 % tn == 0 and n % tn == 0
    in_specs = [pl.BlockSpec((tm, d), lambda i, j: (i, 0)),
                pl.BlockSpec((None,) * len(lead) + (d, tn), lambda i, j: lead + (0, j + col0 // tn))]
    args = [hx, w_in]
    if rope is None:
        body, name = _proj_gelu_kernel, "in_proj_gelu"
    else:
        cosf, sinf, seq = rope
        tiles_per_seq = seq // tm
        tab = pl.BlockSpec((tm, HEAD_DIM), lambda i, j: (i % tiles_per_seq, 0))
        in_specs += [tab, tab]
        args += [cosf, sinf]
        body, name = functools.partial(_proj_rope_kernel, heads_per_tile=tn // HEAD_DIM), "in_proj_rope"
    return pl.pallas_call(
        body,
        grid=(r // tm, n // tn),
        in_specs=in_specs,
        out_specs=pl.BlockSpec((tm, tn), lambda i, j: (i, j)),
        out_shape=jax.ShapeDtypeStruct((r, n), BF16),
        compiler_params=_cparams(("arbitrary", "arbitrary")),
        name=name,
    )(*args)


def _mixer_out_kernel(sink_ref, gu_ref, gv_ref, q_ref, kp_ref, kc_ref, kn_ref, vp_ref, vc_ref, vn_ref,
                      kctx_ref, vctx_ref, ws_ref, bs_ref, gmix_ref, band_ref, edge_ref,
                      wout_ref, res_ref, mu_in_ref, rstd_in_ref, gain_in_ref, bias_in_ref, gate_ref,
                      gain_ref, bias_ref, shift_ref, scale_ref,
                      z_ref, mu_ref, rstd_ref, hm_ref, ybuf, ymix,
                      *, n_groups, n_kv, q_per_kv, seq, n_blocks, alpha):
    t = pl.program_id(0)
    blk = jnp.minimum(t, n_blocks - 1) % (seq // BLOCK)
    slot = t % 2
    mlp_w = n_groups * HEAD_DIM
    attn_w = n_kv * q_per_kv * HEAD_DIM

    @pl.when(t == 0)
    def _():
        ymix[1] = jnp.zeros(ymix.shape[1:], ymix.dtype)

    prev_mix = ymix[1 - slot]
    chunk_w = wout_ref.shape[1] // OUT_PROJ_CHUNKS

    def stage_b_chunk(c):
        cols = slice(c * chunk_w, (c + 1) * chunk_w)
        mix = jnp.dot(prev_mix, wout_ref[:, cols], preferred_element_type=F32)
        h = ((res_ref[:, cols] - mu_in_ref[...]) * rstd_in_ref[...] * gain_in_ref[:, cols]
             + bias_in_ref[:, cols])
        z_ref[:, cols] = alpha * h + gate_ref[:, cols] * mix

    def stage_b_stats():
        mul = gain_ref[...] * (1.0 + scale_ref[...])
        add = bias_ref[...] * (1.0 + scale_ref[...]) + shift_ref[...]
        for r0 in range(0, BLOCK, LN_ROW_CHUNK):
            rows = slice(r0, r0 + LN_ROW_CHUNK)
            z = z_ref[rows, :]
            mu = jnp.mean(z, axis=-1, keepdims=True)
            zc = z - mu
            var = jnp.mean(zc * zc, axis=-1, keepdims=True)
            rstd = lax.rsqrt(var + EPS)
            mu_ref[rows, :] = mu
            rstd_ref[rows, :] = rstd
            hm_ref[rows, :] = (zc * rstd * mul + add).astype(BF16)

    stage_b_chunk(0)

    gv = gv_ref[...].astype(F32)
    mu = jnp.mean(gv, axis=-1, keepdims=True)
    vc_ = gv - mu
    var = jnp.mean(vc_ * vc_, axis=-1, keepdims=True)
    vstd = (vc_ * lax.rsqrt(var + EPS)).astype(BF16)
    ss_mlp = jnp.zeros((CHUNK, 1), F32)
    for g in range(n_groups):
        sl = slice(g * HEAD_DIM, (g + 1) * HEAD_DIM)
        mixed = jnp.dot(ws_ref[g], vstd[:, sl], preferred_element_type=F32) + bs_ref[:, g:g + 1]
        ym = gu_ref[:, sl].astype(F32) * mixed
        ss_mlp = ss_mlp + jnp.sum(ym * ym, axis=-1, keepdims=True)
        ybuf[:, sl] = ym

    stage_b_chunk(1)

    edge = (jnp.where(blk == 0, edge_ref[0:1, :], 0.0)
            + jnp.where(blk == seq // BLOCK - 1, edge_ref[1:2, :], 0.0))
    bias = band_ref[...] + edge
    log2e = np.float32(np.log2(np.e))
    scale2 = np.float32(HEAD_DIM ** -0.5) * log2e
    head = lambda ref, qh: ref[:, qh * HEAD_DIM:(qh + 1) * HEAD_DIM]
    n_keys = band_ref.shape[1]
    ones = jnp.ones((n_keys, HEAD_DIM), BF16)
    ss_attn = jnp.zeros((BLOCK, 1), F32)
    next_chunk = 2
    for h0 in range(0, n_kv, ATTN_HEAD_GROUP):
        heads = range(h0, h0 + ATTN_HEAD_GROUP)
        q_all = jnp.stack([jnp.concatenate([head(q_ref, h * q_per_kv + g) for g in range(q_per_kv)], axis=0)
                           for h in heads])
        k_all = jnp.stack([jnp.concatenate([head(r, h) for r in (kctx_ref, kp_ref, kc_ref, kn_ref)], axis=0)
                           for h in heads])
        v_all = jnp.stack([jnp.concatenate(
            [jnp.concatenate([head(r, h) for r in (vctx_ref, vp_ref, vc_ref, vn_ref)], axis=0), ones], axis=1)
            for h in heads])
        sink2 = jnp.stack([jnp.concatenate([jnp.full((BLOCK, 1), sink_ref[h * q_per_kv + g] * log2e, F32)
                                            for g in range(q_per_kv)], axis=0) for h in heads])
        s = jnp.einsum('hqd,hkd->hqk', q_all, k_all, preferred_element_type=F32) * scale2 + bias[None]
        m = jnp.maximum(jnp.max(s, axis=-1, keepdims=True), sink2)
        stage_b_chunk(next_chunk)
        p = jnp.exp2(s - m)
        pv = jnp.einsum('hqk,hkd->hqd', p.astype(BF16), v_all, preferred_element_type=F32)
        stage_b_chunk(next_chunk + 1)
        next_chunk += 2
        denom = pv[:, :, HEAD_DIM:HEAD_DIM + 1] + jnp.exp2(sink2 - m)
        o = pv[:, :, :HEAD_DIM] * (1.0 / denom)
        for i, h in enumerate(heads):
            for g in range(q_per_kv):
                qh = h * q_per_kv + g
                og = o[i, g * BLOCK:(g + 1) * BLOCK, :]
                ss_attn = ss_attn + jnp.sum(og * og, axis=-1, keepdims=True)
                ybuf[:, mlp_w + qh * HEAD_DIM:mlp_w + (qh + 1) * HEAD_DIM] = og

    stage_b_chunk(6)

    rs_mlp = lax.rsqrt(ss_mlp * np.float32(1.0 / mlp_w) + EPS)
    rs_attn = lax.rsqrt(ss_attn * np.float32(1.0 / attn_w) + EPS)
    ymix[slot, :, :mlp_w] = (ybuf[:, :mlp_w] * rs_mlp * gmix_ref[:, :mlp_w]).astype(BF16)
    ymix[slot, :, mlp_w:] = (ybuf[:, mlp_w:] * rs_attn * gmix_ref[:, mlp_w:]).astype(BF16)

    stage_b_chunk(7)
    stage_b_stats()


def _band_masks(q_per_kv, c_len):
    n_keys = c_len + 3 * BLOCK
    r = np.arange(q_per_kv * BLOCK)[:, None] % BLOCK
    col = np.arange(n_keys)[None, :]
    k_off = col - (c_len + BLOCK)
    in_band = (col < c_len) | (np.abs(k_off - r) <= WINDOW)
    band = np.where(in_band, 0.0, -np.inf).astype(np.float32)
    is_prev = (col >= c_len) & (col < c_len + BLOCK)
    is_next = col >= c_len + 2 * BLOCK
    edge = np.where(np.concatenate([is_prev, is_next], axis=0), -np.inf, 0.0).astype(np.float32)
    return jnp.asarray(band), jnp.asarray(edge)


def _mixer_out(gg, qk, vv, kvctx, w_s, b_sT, sink, g_mix, w_out, res, norm_in, gate, ln, next_mod, alpha,
               batch, seq, n_groups, n_kv, q_per_kv, c_len):
    nb = seq // BLOCK
    n_blocks = batch * nb
    mlp_w = n_groups * HEAD_DIM
    attn_w = n_kv * q_per_kv * HEAD_DIM
    d = w_out.shape[-1]
    kv_w = n_kv * HEAD_DIM
    n_keys = c_len + 3 * BLOCK
    band, edge = _band_masks(q_per_kv, c_len)
    mu_in, rstd_in, gain_in, bias_in = norm_in
    cur = lambda t: jnp.minimum(t, n_blocks - 1)
    prev = lambda t: cur(t) - jnp.where(cur(t) % nb == 0, 0, 1)
    nxt = lambda t: cur(t) + jnp.where(cur(t) % nb == nb - 1, 0, 1)
    lag = lambda t: jnp.maximum(t - 1, 0)
    wide = lambda c: pl.BlockSpec((BLOCK, mlp_w), lambda t: (cur(t), c))
    kvs = lambda rfn, c: pl.BlockSpec((BLOCK, kv_w), lambda t: (rfn(t), c))
    k_col = attn_w // kv_w
    const = lambda shape: pl.BlockSpec(shape, lambda t: (0,) * len(shape), pipeline_mode=pl.Buffered(1))
    lag_row =pl.BlockSpec((BLOCK, d), lambda t: (lag(t), 0))
    lag_stat = pl.BlockSpec((BLOCK, 1), lambda t: (lag(t), 0))
    lag_vec = pl.BlockSpec((None, 1, d), lambda t: (lag(t) // nb, 0, 0))
    return pl.pallas_call(
        functools.partial(_mixer_out_kernel, n_groups=n_groups, n_kv=n_kv, q_per_kv=q_per_kv, seq=seq,
                          n_blocks=n_blocks, alpha=alpha),
        grid=(n_blocks + 1,),
        in_specs=[pl.BlockSpec(memory_space=pltpu.SMEM),
                  wide(0), wide(1), pl.BlockSpec((BLOCK, attn_w), lambda t: (cur(t), 0)),
                  kvs(prev, k_col), kvs(cur, k_col), kvs(nxt, k_col),
                  kvs(prev, 0), kvs(cur, 0), kvs(nxt, 0),
                  pl.BlockSpec((c_len, kv_w), lambda t: (cur(t) // nb, 0)),
                  pl.BlockSpec((c_len, kv_w), lambda t: (cur(t) // nb, 1)),
                  const((n_groups, CHUNK, CHUNK)), const((CHUNK, n_groups)), const((1, mlp_w + attn_w)),
                  const((q_per_kv * BLOCK, n_keys)), const((2, n_keys)),
                  const((mlp_w + attn_w, d)),
                  lag_row, lag_stat, lag_stat, const((1, d)), const((1, d)), lag_vec,
                  const((1, d)), const((1, d)), lag_vec, lag_vec],
        out_specs=[lag_row, lag_stat, lag_stat, lag_row],
        out_shape=[jax.ShapeDtypeStruct((n_blocks * BLOCK, d), F32),
                   jax.ShapeDtypeStruct((n_blocks * BLOCK, 1), F32),
                   jax.ShapeDtypeStruct((n_blocks * BLOCK, 1), F32),
                   jax.ShapeDtypeStruct((n_blocks * BLOCK, d), BF16)],
        scratch_shapes=[pltpu.VMEM((BLOCK, mlp_w + attn_w), F32),
                        pltpu.VMEM((2, BLOCK, mlp_w + attn_w), BF16)],
        compiler_params=pltpu.CompilerParams(dimension_semantics=("arbitrary",),
                                             vmem_limit_bytes=MIXER_VMEM_LIMIT_BYTES),
        name="mixer_out",
    )(sink, gg, gg, qk, qk, qk, qk, vv, vv, vv, kvctx, kvctx, w_s, b_sT, g_mix, band, edge,
      w_out, res, mu_in, rstd_in, gain_in.reshape(1, d), bias_in.reshape(1, d), gate,
      ln[0].reshape(1, d), ln[1].reshape(1, d), next_mod[0], next_mod[1])


def _rope_tables(n):
    rows = n // GRID_W
    row = jnp.broadcast_to(jnp.arange(rows, dtype=F32)[:, None], (rows, GRID_W)).reshape(n)
    col = jnp.broadcast_to(jnp.arange(GRID_W, dtype=F32)[None, :], (rows, GRID_W)).reshape(n)
    n_freq = HEAD_DIM // 4
    inv_freq = ROPE_BASE ** (-jnp.arange(n_freq, dtype=F32) / n_freq)
    ang = jnp.concatenate([row[:, None] * inv_freq, col[:, None] * inv_freq], axis=-1)
    cs, sn = jnp.cos(ang), jnp.sin(ang)
    return jnp.concatenate([cs, cs], axis=-1), jnp.concatenate([-sn, sn], axis=-1)


def kernel(x, c, ctx, c_ctx, w_ada, b_ada, w_ffn_gate, w_ffn_up, w_ffn_down, w_in, w_spatial, b_spatial,
           sink_logit, g_mix, w_out, ln_gain, ln_bias):
    b, n, d = x.shape
    c_len = ctx.shape[1]
    depth = w_ada.shape[0]
    d_ff = w_ffn_gate.shape[-1]
    n_groups = w_spatial.shape[1]
    mlp_w = n_groups * HEAD_DIM
    n_q = sink_logit.shape[1]
    attn_w = n_q * HEAD_DIM
    kv_w = (w_in.shape[-1] - 2 * mlp_w - attn_w) // 2
    n_kv = kv_w // HEAD_DIM
    q_per_kv = n_q // n_kv
    assert depth == 1 and b + 1 <= 8
    alpha = float((2.0 * depth) ** 0.25)
    ctx_seg = b

    x_seg = lambda r0: r0 // n
    c_seg = lambda r0: ctx_seg

    cosf, sinf = _rope_tables(n)
    x2 = x.reshape(b * n, d)
    ctx2 = ctx.reshape(b * c_len, d)

    layer = 0
    c8 = jnp.concatenate([c, c_ctx[None, :], jnp.zeros((8 - b - 1, d), F32)], axis=0)
    m = _ada(c8, w_ada[layer], b_ada[layer]).reshape(8, N_SUB, N_MOD, d)
    mod = lambda s, k: m[:, s, k, :][:, None, :]

    w_down_b = w_ffn_down.astype(BF16)

    def ffn_z(res, hm, idx, s, seg_fn, tm_up, norm=None):
        a = _swiglu_up(hm, w_ffn_gate, w_ffn_up, (layer, idx), tm=tm_up, tn=FF_TILE)
        return _matmul_resid(a, w_down_b, (layer, idx), res, mod(s, 2), seg_fn, alpha, 0.5,
                             tm=1024, tn=512, tk=d_ff // 2, name="ffn_down", norm=norm)

    xm = _modcast(x2, mod(0, 0), mod(0, 1), x_seg)
    cm = _modcast(ctx2, mod(0, 0), mod(0, 1), c_seg)
    mix_mod = (mod(1, 0), mod(1, 1))
    ln0 = (ln_gain[layer, 0], ln_bias[layer, 0])
    z1 = ffn_z(x2, xm, 0, 0, x_seg, UP_TM)
    z1c = ffn_z(ctx2, cm, 0, 0, c_seg, 1024)
    mu1, rstd1, hx = _ln(z1, *ln0, seg_fn=x_seg, next_mod=mix_mod)
    _, _, hc = _ln(z1c, *ln0, seg_fn=c_seg, next_mod=mix_mod)

    gg = _proj(hx, w_in, (layer,), 0, 2 * mlp_w)
    qk = _proj(hx, w_in, (layer,), 2 * mlp_w, attn_w + kv_w, rope=(cosf, sinf, n))
    vv = _matmul(hx, w_in, (layer,), BF16, tm=1024, tn=512, tk=d, name="in_proj_v",
                 col0=2 * mlp_w + attn_w + kv_w, n=kv_w)
    kvctx = _matmul(hc, w_in, (layer,), BF16, tm=b * c_len, tn=kv_w, tk=d, name="ctx_kv",
                    col0=2 * mlp_w + attn_w)
    ln1 = (ln_gain[layer, 1], ln_bias[layer, 1])
    z2, mu2, rstd2, hm2 = _mixer_out(
        gg, qk, vv, kvctx, w_spatial[layer].astype(BF16), b_spatial[layer].T, sink_logit[layer],
        g_mix[layer].reshape(1, mlp_w + attn_w), w_out[layer].astype(BF16), z1, (mu1, rstd1) + ln0, mod(1, 2),
        ln1, (mod(2, 0), mod(2, 1)), alpha, b, n, n_groups, n_kv, q_per_kv, c_len)

    z3 = ffn_z(z2, hm2, 1, 2, x_seg, UP_TM, norm=(mu2, rstd2) + ln1)
    out = _ln(z3, ln_gain[layer, 2], ln_bias[layer, 2])
    return out.reshape(b, n, d)
```

```python
import functools

import jax
import jax.numpy as jnp
import numpy as np
from jax import lax
from jax.experimental import pallas as pl
from jax.experimental.pallas import tpu as pltpu

F32 = jnp.float32
BF16 = jnp.bfloat16

HEAD_DIM = 128
CHUNK = 128
WINDOW = 128
BLOCK = 128
GRID_W = 64
ROPE_BASE = 10000.0
EPS = 1e-6
N_SUB = 3
N_MOD = 3

VMEM_LIMIT_BYTES = 58 * 1024 * 1024
MIXER_VMEM_LIMIT_BYTES = 62 * 1024 * 1024
FF_TILE = 256
UP_TM = 2048
ATTN_HEAD_GROUP = 2
OUT_PROJ_CHUNKS = 8
LN_ROW_CHUNK = 16


def _cparams(sem):
    return pltpu.CompilerParams(dimension_semantics=sem, vmem_limit_bytes=VMEM_LIMIT_BYTES)


def _ada_kernel(c_ref, w_ref, b_ref, o_ref):
    c = c_ref[...]
    s = (c * jax.nn.sigmoid(c)).astype(BF16)
    o_ref[...] = jnp.dot(s, w_ref[...].astype(BF16), preferred_element_type=F32) + b_ref[...]


def _ada(c8, w_ada, b_ada, tn=512):
    d, n = w_ada.shape
    return pl.pallas_call(
        _ada_kernel,
        grid=(n // tn,),
        in_specs=[pl.BlockSpec((8, d), lambda j: (0, 0)),
                  pl.BlockSpec((d, tn), lambda j: (0, j)),
                  pl.BlockSpec((1, tn), lambda j: (0, j))],
        out_specs=pl.BlockSpec((8, tn), lambda j: (0, j)),
        out_shape=jax.ShapeDtypeStruct((8, n), F32),
        compiler_params=_cparams(("arbitrary",)),
        name="ada",
    )(c8, w_ada, b_ada.reshape(1, n))


def _modcast_kernel(x_ref, shift_ref, scale_ref, o_ref):
    o_ref[...] = (x_ref[...] * (1.0 + scale_ref[...]) + shift_ref[...]).astype(BF16)


def _modcast(x2d, shift, scale, seg_fn, tm=256):
    r, d = x2d.shape
    vec = pl.BlockSpec((None, 1, d), lambda i: (seg_fn(i * tm), 0, 0))
    return pl.pallas_call(
        _modcast_kernel,
        grid=(r // tm,),
        in_specs=[pl.BlockSpec((tm, d), lambda i: (i, 0)), vec, vec],
        out_specs=pl.BlockSpec((tm, d), lambda i: (i, 0)),
        out_shape=jax.ShapeDtypeStruct((r, d), BF16),
        compiler_params=_cparams(("arbitrary",)),
        name="modcast",
    )(x2d, shift, scale)


def _swiglu_up_kernel(x_ref, wg_ref, wu_ref, o_ref):
    x = x_ref[...]
    g = jnp.dot(x, wg_ref[...].astype(BF16), preferred_element_type=F32)
    u = jnp.dot(x, wu_ref[...].astype(BF16), preferred_element_type=F32)
    o_ref[...] = (g * jax.nn.sigmoid(g) * u).astype(BF16)


def _swiglu_up(xm, w_gate, w_up, lead, tm, tn):
    r, d = xm.shape
    f = w_gate.shape[-1]
    wspec = pl.BlockSpec((None,) * len(lead) + (d, tn), lambda i, j: lead + (0, j))
    return pl.pallas_call(
        _swiglu_up_kernel,
        grid=(r // tm, f // tn),
        in_specs=[pl.BlockSpec((tm, d), lambda i, j: (i, 0)), wspec, wspec],
        out_specs=pl.BlockSpec((tm, tn), lambda i, j: (i, j)),
        out_shape=jax.ShapeDtypeStruct((r, f), BF16),
        compiler_params=_cparams(("arbitrary", "arbitrary")),
        name="swiglu_up",
    )(xm, w_gate, w_up)


def _mm_kernel(a_ref, b_ref, o_ref, *, nk):
    d = jnp.dot(a_ref[...], b_ref[...].astype(BF16), preferred_element_type=F32)
    if nk == 1:
        o_ref[...] = d.astype(o_ref.dtype)
    else:
        k = pl.program_id(2)

        @pl.when(k == 0)
        def _():
            o_ref[...] = d

        @pl.when(k > 0)
        def _():
            o_ref[...] += d


def _matmul(a, b, lead, out_dtype, tm, tn, tk, name, col0=0, n=None):
    r, kk = a.shape
    n = b.shape[-1] - col0 if n is None else n
    nk = kk // tk
    assert nk * tk == kk and col0 % tn == 0 and n % tn == 0
    assert nk == 1 or out_dtype == F32
    return pl.pallas_call(
        functools.partial(_mm_kernel, nk=nk),
        grid=(r // tm, n // tn, nk),
        in_specs=[pl.BlockSpec((tm, tk), lambda i, j, k: (i, k)),
                  pl.BlockSpec((None,) * len(lead) + (tk, tn), lambda i, j, k: lead + (k, j + col0 // tn))],
        out_specs=pl.BlockSpec((tm, tn), lambda i, j, k: (i, j)),
        out_shape=jax.ShapeDtypeStruct((r, n), out_dtype),
        compiler_params=_cparams(("arbitrary", "arbitrary", "arbitrary")),
        name=name,
    )(a, b)


def _mm_resid_kernel(*refs, nk, alpha, coef, normed):
    if normed:
        a_ref, b_ref, res_ref, gate_ref, mu_ref, rstd_ref, gain_ref, bias_ref, o_ref = refs
    else:
        a_ref, b_ref, res_ref, gate_ref, o_ref = refs
    def partial_product():
        return jnp.dot(a_ref[...], b_ref[...].astype(BF16), preferred_element_type=F32)

    def finish(y):
        h = res_ref[...]
        if normed:
            h = (h - mu_ref[...]) * rstd_ref[...] * gain_ref[...] + bias_ref[...]
        o_ref[...] = alpha * h + (coef * gate_ref[...]) * y

    if nk == 1:
        finish(partial_product())
    else:
        k = pl.program_id(2)

        @pl.when(k == 0)
        def _():
            o_ref[...] = partial_product()

        if nk > 2:
            @pl.when((k > 0) & (k < nk - 1))
            def _():
                o_ref[...] += partial_product()

        @pl.when(k == nk - 1)
        def _():
            finish(o_ref[...] + partial_product())


def _matmul_resid(a, b, lead, res, gate, seg_fn, alpha, coef, tm, tn, tk, name, norm=None):
    r, kk = a.shape
    n = b.shape[-1]
    nk = kk // tk
    assert nk * tk == kk and nk >= 1 and n % tn == 0
    tile = pl.BlockSpec((tm, tn), lambda i, j, k: (i, j))
    vec = pl.BlockSpec((None, 1, tn), lambda i, j, k: (seg_fn(i * tm), 0, j))
    in_specs = [pl.BlockSpec((tm, tk), lambda i, j, k: (i, k)),
                pl.BlockSpec((None,) * len(lead) + (tk, tn), lambda i, j, k: lead + (k, j)),
                tile, vec]
    args = [a, b, res, gate]
    if norm is not None:
        mu, rstd, gain, bias = norm
        stat = pl.BlockSpec((tm, 1), lambda i, j, k: (i, 0))
        par = pl.BlockSpec((1, tn), lambda i, j, k: (0, j))
        in_specs += [stat, stat, par, par]
        args += [mu, rstd, gain.reshape(1, n), bias.reshape(1, n)]
    return pl.pallas_call(
        functools.partial(_mm_resid_kernel, nk=nk, alpha=alpha, coef=coef, normed=norm is not None),
        grid=(r // tm, n // tn, nk),
        in_specs=in_specs,
        out_specs=tile,
        out_shape=jax.ShapeDtypeStruct((r, n), F32),
        compiler_params=_cparams(("arbitrary", "arbitrary", "arbitrary")),
        name=name,
    )(*args)


def _ln_kernel(*refs, final):
    if final:
        z_ref, gain_ref, bias_ref, o_ref = refs
    else:
        z_ref, gain_ref, bias_ref, shift_ref, scale_ref, mu_ref, rstd_ref, om_ref = refs
    if final:
        mul, add = gain_ref[...], bias_ref[...]
    else:
        mul = gain_ref[...] * (1.0 + scale_ref[...])
        add = bias_ref[...] * (1.0 + scale_ref[...]) + shift_ref[...]
    for r0 in range(0, z_ref.shape[0], LN_ROW_CHUNK):
        rows = slice(r0, r0 + LN_ROW_CHUNK)
        z = z_ref[rows, :]
        mu = jnp.mean(z, axis=-1, keepdims=True)
        zc = z - mu
        var = jnp.mean(zc * zc, axis=-1, keepdims=True)
        rstd = lax.rsqrt(var + EPS)
        y = zc * rstd * mul + add
        if final:
            o_ref[rows, :] = y
        else:
            mu_ref[rows, :] = mu
            rstd_ref[rows, :] = rstd
            om_ref[rows, :] = y.astype(BF16)


def _ln(z, gain, bias, seg_fn=None, next_mod=None, tm=256):
    r, d = z.shape
    row = pl.BlockSpec((tm, d), lambda i: (i, 0))
    full = pl.BlockSpec((1, d), lambda i: (0, 0))
    final = next_mod is None
    in_specs = [row, full, full]
    args = [z, gain.reshape(1, d), bias.reshape(1, d)]
    if final:
        out_specs = row
        out_shape = jax.ShapeDtypeStruct((r, d), F32)
    else:
        vec = pl.BlockSpec((None, 1, d), lambda i: (seg_fn(i * tm), 0, 0))
        stat = pl.BlockSpec((tm, 1), lambda i: (i, 0))
        in_specs += [vec, vec]
        args += list(next_mod)
        out_specs = [stat, stat, row]
        out_shape = [jax.ShapeDtypeStruct((r, 1), F32), jax.ShapeDtypeStruct((r, 1), F32),
                     jax.ShapeDtypeStruct((r, d), BF16)]
    return pl.pallas_call(
        functools.partial(_ln_kernel, final=final),
        grid=(r // tm,),
        in_specs=in_specs,
        out_specs=out_specs,
        out_shape=out_shape,
        compiler_params=_cparams(("arbitrary",)),
        name="ln_final" if final else "ln_stats",
    )(*args)


def _proj_gelu_kernel(x_ref, w_ref, o_ref):
    acc = jnp.dot(x_ref[...], w_ref[...].astype(BF16), preferred_element_type=F32)
    o_ref[...] = (0.5 * acc * (1.0 + lax.erf(acc * np.float32(np.sqrt(0.5))))).astype(BF16)


def _proj_rope_kernel(x_ref, w_ref, cos_ref, sin_ref, o_ref, *, heads_per_tile):
    acc = jnp.dot(x_ref[...], w_ref[...].astype(BF16), preferred_element_type=F32)
    cs = cos_ref[...]
    sn = sin_ref[...]
    for hh in range(heads_per_tile):
        t = acc[:, hh * HEAD_DIM:(hh + 1) * HEAD_DIM]
        rot = pltpu.roll(t, HEAD_DIM // 2, 1)
        o_ref[:, hh * HEAD_DIM:(hh + 1) * HEAD_DIM] = (t * cs + rot * sn).astype(BF16)


def _proj(hx, w_in, lead, col0, n, rope=None, tm=1024, tn=512):
    r, d = hx.shape
    assert col0 % tn == 0 and n % tn == 0
    in_specs = [pl.BlockSpec((tm, d), lambda i, j: (i, 0)),
                pl.BlockSpec((None,) * len(lead) + (d, tn), lambda i, j: lead + (0, j + col0 // tn))]
    args = [hx, w_in]
    if rope is None:
        body, name = _proj_gelu_kernel, "in_proj_gelu"
    else:
        cosf, sinf, seq = rope
        tiles_per_seq = seq // tm
        tab = pl.BlockSpec((tm, HEAD_DIM), lambda i, j: (i % tiles_per_seq, 0))
        in_specs += [tab, tab]
        args += [cosf, sinf]
        body, name = functools.partial(_proj_rope_kernel, heads_per_tile=tn // HEAD_DIM), "in_proj_rope"
    return pl.pallas_call(
        body,
        grid=(r // tm, n // tn),
        in_specs=in_specs,
        out_specs=pl.BlockSpec((tm, tn), lambda i, j: (i, j)),
        out_shape=jax.ShapeDtypeStruct((r, n), BF16),
        compiler_params=_cparams(("arbitrary", "arbitrary")),
        name=name,
    )(*args)


def _mixer_out_kernel(sink_ref, gu_ref, gv_ref, q_ref, kp_ref, kc_ref, kn_ref, vp_ref, vc_ref, vn_ref,
                      kctx_ref, vctx_ref, ws_ref, bs_ref, gmix_ref, band_ref, edge_ref,
                      wout_ref, res_ref, mu_in_ref, rstd_in_ref, gain_in_ref, bias_in_ref, gate_ref,
                      gain_ref, bias_ref, shift_ref, scale_ref,
                      z_ref, mu_ref, rstd_ref, hm_ref, ybuf, ymix,
                      *, n_groups, n_kv, q_per_kv, seq, n_blocks, alpha):
    t = pl.program_id(0)
    blk = jnp.minimum(t, n_blocks - 1) % (seq // BLOCK)
    slot = t % 2
    mlp_w = n_groups * HEAD_DIM
    attn_w = n_kv * q_per_kv * HEAD_DIM

    @pl.when(t == 0)
    def _():
        ymix[1] = jnp.zeros(ymix.shape[1:], ymix.dtype)

    prev_mix = ymix[1 - slot]
    chunk_w = wout_ref.shape[1] // OUT_PROJ_CHUNKS

    def stage_b_chunk(c):
        cols = slice(c * chunk_w, (c + 1) * chunk_w)
        mix = jnp.dot(prev_mix, wout_ref[:, cols], preferred_element_type=F32)
        h = ((res_ref[:, cols] - mu_in_ref[...]) * rstd_in_ref[...] * gain_in_ref[:, cols]
             + bias_in_ref[:, cols])
        z_ref[:, cols] = alpha * h + gate_ref[:, cols] * mix

    def stage_b_stats():
        mul = gain_ref[...] * (1.0 + scale_ref[...])
        add = bias_ref[...] * (1.0 + scale_ref[...]) + shift_ref[...]
        for r0 in range(0, BLOCK, LN_ROW_CHUNK):
            rows = slice(r0, r0 + LN_ROW_CHUNK)
            z = z_ref[rows, :]
            mu = jnp.mean(z, axis=-1, keepdims=True)
            zc = z - mu
            var = jnp.mean(zc * zc, axis=-1, keepdims=True)
            rstd = lax.rsqrt(var + EPS)
            mu_ref[rows, :] = mu
            rstd_ref[rows, :] = rstd
            hm_ref[rows, :] = (zc * rstd * mul + add).astype(BF16)

    stage_b_chunk(0)

    gv = gv_ref[...].astype(F32)
    mu = jnp.mean(gv, axis=-1, keepdims=True)
    vc_ = gv - mu
    var = jnp.mean(vc_ * vc_, axis=-1, keepdims=True)
    vstd = (vc_ * lax.rsqrt(var + EPS)).astype(BF16)
    ss_mlp = jnp.zeros((CHUNK, 1), F32)
    for g in range(n_groups):
        sl = slice(g * HEAD_DIM, (g + 1) * HEAD_DIM)
        mixed = jnp.dot(ws_ref[g], vstd[:, sl], preferred_element_type=F32) + bs_ref[:, g:g + 1]
        ym = gu_ref[:, sl].astype(F32) * mixed
        ss_mlp = ss_mlp + jnp.sum(ym * ym, axis=-1, keepdims=True)
        ybuf[:, sl] = ym

    stage_b_chunk(1)

    edge = (jnp.where(blk == 0, edge_ref[0:1, :], 0.0)
            + jnp.where(blk == seq // BLOCK - 1, edge_ref[1:2, :], 0.0))
    bias = band_ref[...] + edge
    log2e = np.float32(np.log2(np.e))
    scale2 = np.float32(HEAD_DIM ** -0.5) * log2e
    head = lambda ref, qh: ref[:, qh * HEAD_DIM:(qh + 1) * HEAD_DIM]
    n_keys = band_ref.shape[1]
    ones = jnp.ones((n_keys, HEAD_DIM), BF16)
    ss_attn = jnp.zeros((BLOCK, 1), F32)
    next_chunk = 2
    for h0 in range(0, n_kv, ATTN_HEAD_GROUP):
        heads = range(h0, h0 + ATTN_HEAD_GROUP)
        q_all = jnp.stack([jnp.concatenate([head(q_ref, h * q_per_kv + g) for g in range(q_per_kv)], axis=0)
                           for h in heads])
        k_all = jnp.stack([jnp.concatenate([head(r, h) for r in (kctx_ref, kp_ref, kc_ref, kn_ref)], axis=0)
                           for h in heads])
        v_all = jnp.stack([jnp.concatenate(
            [jnp.concatenate([head(r, h) for r in (vctx_ref, vp_ref, vc_ref, vn_ref)], axis=0), ones], axis=1)
            for h in heads])
        sink2 = jnp.stack([jnp.concatenate([jnp.full((BLOCK, 1), sink_ref[h * q_per_kv + g] * log2e, F32)
                                            for g in range(q_per_kv)], axis=0) for h in heads])
        s = jnp.einsum('hqd,hkd->hqk', q_all, k_all, preferred_element_type=F32) * scale2 + bias[None]
        m = jnp.maximum(jnp.max(s, axis=-1, keepdims=True), sink2)
        stage_b_chunk(next_chunk)
        p = jnp.exp2(s - m)
        pv = jnp.einsum('hqk,hkd->hqd', p.astype(BF16), v_all, preferred_element_type=F32)
        stage_b_chunk(next_chunk + 1)
        next_chunk += 2
        denom = pv[:, :, HEAD_DIM:HEAD_DIM + 1] + jnp.exp2(sink2 - m)
        o = pv[:, :, :HEAD_DIM] * (1.0 / denom)
        for i, h in enumerate(heads):
            for g in range(q_per_kv):
                qh = h * q_per_kv + g
                og = o[i, g * BLOCK:(g + 1) * BLOCK, :]
                ss_attn = ss_attn + jnp.sum(og * og, axis=-1, keepdims=True)
                ybuf[:, mlp_w + qh * HEAD_DIM:mlp_w + (qh + 1) * HEAD_DIM] = og

    stage_b_chunk(6)

    rs_mlp = lax.rsqrt(ss_mlp * np.float32(1.0 / mlp_w) + EPS)
    rs_attn = lax.rsqrt(ss_attn * np.float32(1.0 / attn_w) + EPS)
    ymix[slot, :, :mlp_w] = (ybuf[:, :mlp_w] * rs_mlp * gmix_ref[:, :mlp_w]).astype(BF16)
    ymix[slot, :, mlp_w:] = (ybuf[:, mlp_w:] * rs_attn * gmix_ref[:, mlp_w:]).astype(BF16)

    stage_b_chunk(7)
    stage_b_stats()


def _band_masks(q_per_kv, c_len):
    n_keys = c_len + 3 * BLOCK
    r = np.arange(q_per_kv * BLOCK)[:, None] % BLOCK
    col = np.arange(n_keys)[None, :]
    k_off = col - (c_len + BLOCK)
    in_band = (col < c_len) | (np.abs(k_off - r) <= WINDOW)
    band = np.where(in_band, 0.0, -np.inf).astype(np.float32)
    is_prev = (col >= c_len) & (col < c_len + BLOCK)
    is_next = col >= c_len + 2 * BLOCK
    edge = np.where(np.concatenate([is_prev, is_next], axis=0), -np.inf, 0.0).astype(np.float32)
    return jnp.asarray(band), jnp.asarray(edge)


def _mixer_out(gg, qk, vv, kvctx, w_s, b_sT, sink, g_mix, w_out, res, norm_in, gate, ln, next_mod, alpha,
               batch, seq, n_groups, n_kv, q_per_kv, c_len):
    nb = seq // BLOCK
    n_blocks = batch * nb
    mlp_w = n_groups * HEAD_DIM
    attn_w = n_kv * q_per_kv * HEAD_DIM
    d = w_out.shape[-1]
    kv_w = n_kv * HEAD_DIM
    n_keys = c_len + 3 * BLOCK
    band, edge = _band_masks(q_per_kv, c_len)
    mu_in, rstd_in, gain_in, bias_in = norm_in
    cur = lambda t: jnp.minimum(t, n_blocks - 1)
    prev = lambda t: cur(t) - jnp.where(cur(t) % nb == 0, 0, 1)
    nxt = lambda t: cur(t) + jnp.where(cur(t) % nb == nb - 1, 0, 1)
    lag = lambda t: jnp.maximum(t - 1, 0)
    wide = lambda c: pl.BlockSpec((BLOCK, mlp_w), lambda t: (cur(t), c))
    kvs = lambda rfn, c: pl.BlockSpec((BLOCK, kv_w), lambda t: (rfn(t), c))
    k_col = attn_w // kv_w
    const = lambda shape: pl.BlockSpec(shape, lambda t: (0,) * len(shape), pipeline_mode=pl.Buffered(1))
    lag_row =pl.BlockSpec((BLOCK, d), lambda t: (lag(t), 0))
    lag_stat = pl.BlockSpec((BLOCK, 1), lambda t: (lag(t), 0))
    lag_vec = pl.BlockSpec((None, 1, d), lambda t: (lag(t) // nb, 0, 0))
    return pl.pallas_call(
        functools.partial(_mixer_out_kernel, n_groups=n_groups, n_kv=n_kv, q_per_kv=q_per_kv, seq=seq,
                          n_blocks=n_blocks, alpha=alpha),
        grid=(n_blocks + 1,),
        in_specs=[pl.BlockSpec(memory_space=pltpu.SMEM),
                  wide(0), wide(1), pl.BlockSpec((BLOCK, attn_w), lambda t: (cur(t), 0)),
                  kvs(prev, k_col), kvs(cur, k_col), kvs(nxt, k_col),
                  kvs(prev, 0), kvs(cur, 0), kvs(nxt, 0),
                  pl.BlockSpec((c_len, kv_w), lambda t: (cur(t) // nb, 0)),
                  pl.BlockSpec((c_len, kv_w), lambda t: (cur(t) // nb, 1)),
                  const((n_groups, CHUNK, CHUNK)), const((CHUNK, n_groups)), const((1, mlp_w + attn_w)),
                  const((q_per_kv * BLOCK, n_keys)), const((2, n_keys)),
                  const((mlp_w + attn_w, d)),
                  lag_row, lag_stat, lag_stat, const((1, d)), const((1, d)), lag_vec,
                  const((1, d)), const((1, d)), lag_vec, lag_vec],
        out_specs=[lag_row, lag_stat, lag_stat, lag_row],
        out_shape=[jax.ShapeDtypeStruct((n_blocks * BLOCK, d), F32),
                   jax.ShapeDtypeStruct((n_blocks * BLOCK, 1), F32),
                   jax.ShapeDtypeStruct((n_blocks * BLOCK, 1), F32),
                   jax.ShapeDtypeStruct((n_blocks * BLOCK, d), BF16)],
        scratch_shapes=[pltpu.VMEM((BLOCK, mlp_w + attn_w), F32),
                        pltpu.VMEM((2, BLOCK, mlp_w + attn_w), BF16)],
        compiler_params=pltpu.CompilerParams(dimension_semantics=("arbitrary",),
                                             vmem_limit_bytes=MIXER_VMEM_LIMIT_BYTES),
        name="mixer_out",
    )(sink, gg, gg, qk, qk, qk, qk, vv, vv, vv, kvctx, kvctx, w_s, b_sT, g_mix, band, edge,
      w_out, res, mu_in, rstd_in, gain_in.reshape(1, d), bias_in.reshape(1, d), gate,
      ln[0].reshape(1, d), ln[1].reshape(1, d), next_mod[0], next_mod[1])


def _rope_tables(n):
    rows = n // GRID_W
    row = jnp.broadcast_to(jnp.arange(rows, dtype=F32)[:, None], (rows, GRID_W)).reshape(n)
    col = jnp.broadcast_to(jnp.arange(GRID_W, dtype=F32)[None, :], (rows, GRID_W)).reshape(n)
    n_freq = HEAD_DIM // 4
    inv_freq = ROPE_BASE ** (-jnp.arange(n_freq, dtype=F32) / n_freq)
    ang = jnp.concatenate([row[:, None] * inv_freq, col[:, None] * inv_freq], axis=-1)
    cs, sn = jnp.cos(ang), jnp.sin(ang)
    return jnp.concatenate([cs, cs], axis=-1), jnp.concatenate([-sn, sn], axis=-1)


def kernel(x, c, ctx, c_ctx, w_ada, b_ada, w_ffn_gate, w_ffn_up, w_ffn_down, w_in, w_spatial, b_spatial,
           sink_logit, g_mix, w_out, ln_gain, ln_bias):
    b, n, d = x.shape
    c_len = ctx.shape[1]
    depth = w_ada.shape[0]
    d_ff = w_ffn_gate.shape[-1]
    n_groups = w_spatial.shape[1]
    mlp_w = n_groups * HEAD_DIM
    n_q = sink_logit.shape[1]
    attn_w = n_q * HEAD_DIM
    kv_w = (w_in.shape[-1] - 2 * mlp_w - attn_w) // 2
    n_kv = kv_w // HEAD_DIM
    q_per_kv = n_q // n_kv
    assert depth == 1 and b + 1 <= 8
    alpha = float((2.0 * depth) ** 0.25)
    ctx_seg = b

    x_seg = lambda r0: r0 // n
    c_seg = lambda r0: ctx_seg

    cosf, sinf = _rope_tables(n)
    x2 = x.reshape(b * n, d)
    ctx2 = ctx.reshape(b * c_len, d)

    layer = 0
    c8 = jnp.concatenate([c, c_ctx[None, :], jnp.zeros((8 - b - 1, d), F32)], axis=0)
    m = _ada(c8, w_ada[layer], b_ada[layer]).reshape(8, N_SUB, N_MOD, d)
    mod = lambda s, k: m[:, s, k, :][:, None, :]

    w_down_b = w_ffn_down.astype(BF16)

    def ffn_z(res, hm, idx, s, seg_fn, tm_up, norm=None):
        a = _swiglu_up(hm, w_ffn_gate, w_ffn_up, (layer, idx), tm=tm_up, tn=FF_TILE)
        return _matmul_resid(a, w_down_b, (layer, idx), res, mod(s, 2), seg_fn, alpha, 0.5,
                             tm=512, tn=512, tk=d_ff, name="ffn_down", norm=norm)

    xm = _modcast(x2, mod(0, 0), mod(0, 1), x_seg)
    cm = _modcast(ctx2, mod(0, 0), mod(0, 1), c_seg)
    mix_mod = (mod(1, 0), mod(1, 1))
    ln0 = (ln_gain[layer, 0], ln_bias[layer, 0])
    z1 = ffn_z(x2, xm, 0, 0, x_seg, UP_TM)
    z1c = ffn_z(ctx2, cm, 0, 0, c_seg, 1024)
    mu1, rstd1, hx = _ln(z1, *ln0, seg_fn=x_seg, next_mod=mix_mod)
    _, _, hc = _ln(z1c, *ln0, seg_fn=c_seg, next_mod=mix_mod)

    gg = _proj(hx, w_in, (layer,), 0, 2 * mlp_w)
    qk = _proj(hx, w_in, (layer,), 2 * mlp_w, attn_w + kv_w, rope=(cosf, sinf, n))
    vv = _matmul(hx, w_in, (layer,), BF16, tm=1024, tn=512, tk=d, name="in_proj_v",
                 col0=2 * mlp_w + attn_w + kv_w, n=kv_w)
    kvctx = _matmul(hc, w_in, (layer,), BF16, tm=b * c_len, tn=kv_w, tk=d, name="ctx_kv",
                    col0=2 * mlp_w + attn_w)
    ln1 = (ln_gain[layer, 1], ln_bias[layer, 1])
    z2, mu2, rstd2, hm2 = _mixer_out(
        gg, qk, vv, kvctx, w_spatial[layer].astype(BF16), b_spatial[layer].T, sink_logit[layer],
        g_mix[layer].reshape(1, mlp_w + attn_w), w_out[layer].astype(BF16), z1, (mu1, rstd1) + ln0, mod(1, 2),
        ln1, (mod(2, 0), mod(2, 1)), alpha, b, n, n_groups, n_kv, q_per_kv, c_len)

    z3 = ffn_z(z2, hm2, 1, 2, x_seg, UP_TM, norm=(mu2, rstd2) + ln1)
    out = _ln(z3, ln_gain[layer, 2], ln_bias[layer, 2])
    return out.reshape(b, n, d)
```

```python
import functools

import jax
import jax.numpy as jnp
import numpy as np
from jax import lax
from jax.experimental import pallas as pl
from jax.experimental.pallas import tpu as pltpu

F32 = jnp.float32
BF16 = jnp.bfloat16

HEAD_DIM = 128
CHUNK = 128
WINDOW = 128
BLOCK = 128
GRID_W = 64
ROPE_BASE = 10000.0
EPS = 1e-6
N_SUB = 3
N_MOD = 3

VMEM_LIMIT_BYTES = 58 * 1024 * 1024
LARGE_VMEM_LIMIT_BYTES = 62 * 1024 * 1024
FF_TILE = 256
UP_TM = 2048
ATTN_HEAD_GROUP = 2
OUT_PROJ_CHUNKS = 8
LN_ROW_CHUNK = 16


def _cparams(sem, vmem_limit_bytes=VMEM_LIMIT_BYTES):
    return pltpu.CompilerParams(dimension_semantics=sem, vmem_limit_bytes=vmem_limit_bytes)


def _ada_kernel(c_ref, w_ref, b_ref, o_ref):
    c = c_ref[...]
    s = (c * jax.nn.sigmoid(c)).astype(BF16)
    o_ref[...] = jnp.dot(s, w_ref[...].astype(BF16), preferred_element_type=F32) + b_ref[...]


def _ada(c8, w_ada, b_ada, tn=512):
    d, n = w_ada.shape
    return pl.pallas_call(
        _ada_kernel,
        grid=(n // tn,),
        in_specs=[pl.BlockSpec((8, d), lambda j: (0, 0)),
                  pl.BlockSpec((d, tn), lambda j: (0, j)),
                  pl.BlockSpec((1, tn), lambda j: (0, j))],
        out_specs=pl.BlockSpec((8, tn), lambda j: (0, j)),
        out_shape=jax.ShapeDtypeStruct((8, n), F32),
        compiler_params=_cparams(("arbitrary",)),
        name="ada",
    )(c8, w_ada, b_ada.reshape(1, n))


def _modcast_kernel(x_ref, shift_ref, scale_ref, o_ref):
    o_ref[...] = (x_ref[...] * (1.0 + scale_ref[...]) + shift_ref[...]).astype(BF16)


def _modcast(x2d, shift, scale, seg_fn, tm=256):
    r, d = x2d.shape
    vec = pl.BlockSpec((None, 1, d), lambda i: (seg_fn(i * tm), 0, 0))
    return pl.pallas_call(
        _modcast_kernel,
        grid=(r // tm,),
        in_specs=[pl.BlockSpec((tm, d), lambda i: (i, 0)), vec, vec],
        out_specs=pl.BlockSpec((tm, d), lambda i: (i, 0)),
        out_shape=jax.ShapeDtypeStruct((r, d), BF16),
        compiler_params=_cparams(("arbitrary",)),
        name="modcast",
    )(x2d, shift, scale)


def _swiglu_up_kernel(x_ref, wg_ref, wu_ref, *rest):
    o_ref = rest[len(rest) // 2]
    x = x_ref[...]
    g = jnp.dot(x, wg_ref[...].astype(BF16), preferred_element_type=F32)
    u = jnp.dot(x, wu_ref[...].astype(BF16), preferred_element_type=F32)
    o_ref[...] = (g * jax.nn.sigmoid(g) * u).astype(BF16)
    n_side = len(rest) // 2
    for src_ref, dst_ref in zip(rest[:n_side], rest[n_side + 1:]):
        dst_ref[...] = src_ref[...].astype(BF16)


def _swiglu_up(xm, w_gate, w_up, lead, tm, tn, side_casts=()):
    r, d = xm.shape
    f = w_gate.shape[-1]
    ni, nj = r // tm, f // tn
    wspec = pl.BlockSpec((None,) * len(lead) + (d, tn), lambda i, j: lead + (0, j))
    side_specs, side_shapes = [], []
    for w in side_casts:
        rows = -(-w.shape[0] // (ni * nj))
        rows = -(-rows // 16) * 16
        n_slabs = w.shape[0] // rows
        assert n_slabs * rows == w.shape[0] and n_slabs <= ni * nj
        side_specs.append(pl.BlockSpec((rows, w.shape[1]),
                                       lambda i, j, n_slabs=n_slabs: (jnp.minimum(i * nj + j, n_slabs - 1), 0)))
        side_shapes.append(jax.ShapeDtypeStruct(w.shape, BF16))
    outs = pl.pallas_call(
        _swiglu_up_kernel,
        grid=(ni, nj),
        in_specs=[pl.BlockSpec((tm, d), lambda i, j: (i, 0)), wspec, wspec] + side_specs,
        out_specs=[pl.BlockSpec((tm, tn), lambda i, j: (i, j))] + side_specs,
        out_shape=[jax.ShapeDtypeStruct((r, f), BF16)] + side_shapes,
        compiler_params=_cparams(("arbitrary", "arbitrary"), LARGE_VMEM_LIMIT_BYTES),
        name="swiglu_up",
    )(xm, w_gate, w_up, *side_casts)
    return outs[0], outs[1:]


def _mm_kernel(a_ref, b_ref, o_ref, *, nk):
    d = jnp.dot(a_ref[...], b_ref[...].astype(BF16), preferred_element_type=F32)
    if nk == 1:
        o_ref[...] = d.astype(o_ref.dtype)
    else:
        k = pl.program_id(2)

        @pl.when(k == 0)
        def _():
            o_ref[...] = d

        @pl.when(k > 0)
        def _():
            o_ref[...] += d


def _matmul(a, b, lead, out_dtype, tm, tn, tk, name, col0=0, n=None):
    r, kk = a.shape
    n = b.shape[-1] - col0 if n is None else n
    nk = kk // tk
    assert nk * tk == kk and col0 % tn == 0 and n % tn == 0
    assert nk == 1 or out_dtype == F32
    return pl.pallas_call(
        functools.partial(_mm_kernel, nk=nk),
        grid=(r // tm, n // tn, nk),
        in_specs=[pl.BlockSpec((tm, tk), lambda i, j, k: (i, k)),
                  pl.BlockSpec((None,) * len(lead) + (tk, tn), lambda i, j, k: lead + (k, j + col0 // tn))],
        out_specs=pl.BlockSpec((tm, tn), lambda i, j, k: (i, j)),
        out_shape=jax.ShapeDtypeStruct((r, n), out_dtype),
        compiler_params=_cparams(("arbitrary", "arbitrary", "arbitrary")),
        name=name,
    )(a, b)


def _mm_resid_kernel(*refs, nk, alpha, coef, normed):
    if normed:
        a_ref, b_ref, res_ref, gate_ref, mu_ref, rstd_ref, gain_ref, bias_ref, o_ref = refs
    else:
        a_ref, b_ref, res_ref, gate_ref, o_ref = refs
    def partial_product():
        return jnp.dot(a_ref[...], b_ref[...].astype(BF16), preferred_element_type=F32)

    def finish(y):
        h = res_ref[...]
        if normed:
            h = (h - mu_ref[...]) * rstd_ref[...] * gain_ref[...] + bias_ref[...]
        o_ref[...] = alpha * h + (coef * gate_ref[...]) * y

    if nk == 1:
        finish(partial_product())
    else:
        k = pl.program_id(2)

        @pl.when(k == 0)
        def _():
            o_ref[...] = partial_product()

        if nk > 2:
            @pl.when((k > 0) & (k < nk - 1))
            def _():
                o_ref[...] += partial_product()

        @pl.when(k == nk - 1)
        def _():
            finish(o_ref[...] + partial_product())


def _matmul_resid(a, b, lead, res, gate, seg_fn, alpha, coef, tm, tn, tk, name, norm=None):
    r, kk = a.shape
    n = b.shape[-1]
    nk = kk // tk
    assert nk * tk == kk and nk >= 1 and n % tn == 0
    tile = pl.BlockSpec((tm, tn), lambda i, j, k: (i, j))
    vec = pl.BlockSpec((None, 1, tn), lambda i, j, k: (seg_fn(i * tm), 0, j))
    in_specs = [pl.BlockSpec((tm, tk), lambda i, j, k: (i, k)),
                pl.BlockSpec((None,) * len(lead) + (tk, tn), lambda i, j, k: lead + (k, j)),
                tile, vec]
    args = [a, b, res, gate]
    if norm is not None:
        mu, rstd, gain, bias = norm
        stat = pl.BlockSpec((tm, 1), lambda i, j, k: (i, 0))
        par = pl.BlockSpec((1, tn), lambda i, j, k: (0, j))
        in_specs += [stat, stat, par, par]
        args += [mu, rstd, gain.reshape(1, n), bias.reshape(1, n)]
    return pl.pallas_call(
        functools.partial(_mm_resid_kernel, nk=nk, alpha=alpha, coef=coef, normed=norm is not None),
        grid=(r // tm, n // tn, nk),
        in_specs=in_specs,
        out_specs=tile,
        out_shape=jax.ShapeDtypeStruct((r, n), F32),
        compiler_params=_cparams(("arbitrary", "arbitrary", "arbitrary")),
        name=name,
    )(*args)


def _ln_kernel(*refs, final):
    if final:
        z_ref, gain_ref, bias_ref, o_ref = refs
    else:
        z_ref, gain_ref, bias_ref, shift_ref, scale_ref, mu_ref, rstd_ref, om_ref = refs
    if final:
        mul, add = gain_ref[...], bias_ref[...]
    else:
        mul = gain_ref[...] * (1.0 + scale_ref[...])
        add = bias_ref[...] * (1.0 + scale_ref[...]) + shift_ref[...]
    for r0 in range(0, z_ref.shape[0], LN_ROW_CHUNK):
        rows = slice(r0, r0 + LN_ROW_CHUNK)
        z = z_ref[rows, :]
        mu = jnp.mean(z, axis=-1, keepdims=True)
        zc = z - mu
        var = jnp.mean(zc * zc, axis=-1, keepdims=True)
        rstd = lax.rsqrt(var + EPS)
        y = zc * rstd * mul + add
        if final:
            o_ref[rows, :] = y
        else:
            mu_ref[rows, :] = mu
            rstd_ref[rows, :] = rstd
            om_ref[rows, :] = y.astype(BF16)


def _ln(z, gain, bias, seg_fn=None, next_mod=None, tm=256):
    r, d = z.shape
    row = pl.BlockSpec((tm, d), lambda i: (i, 0))
    full = pl.BlockSpec((1, d), lambda i: (0, 0))
    final = next_mod is None
    in_specs = [row, full, full]
    args = [z, gain.reshape(1, d), bias.reshape(1, d)]
    if final:
        out_specs = row
        out_shape = jax.ShapeDtypeStruct((r, d), F32)
    else:
        vec = pl.BlockSpec((None, 1, d), lambda i: (seg_fn(i * tm), 0, 0))
        stat = pl.BlockSpec((tm, 1), lambda i: (i, 0))
        in_specs += [vec, vec]
        args += list(next_mod)
        out_specs = [stat, stat, row]
        out_shape = [jax.ShapeDtypeStruct((r, 1), F32), jax.ShapeDtypeStruct((r, 1), F32),
                     jax.ShapeDtypeStruct((r, d), BF16)]
    return pl.pallas_call(
        functools.partial(_ln_kernel, final=final),
        grid=(r // tm,),
        in_specs=in_specs,
        out_specs=out_specs,
        out_shape=out_shape,
        compiler_params=_cparams(("arbitrary",)),
        name="ln_final" if final else "ln_stats",
    )(*args)


def _proj_gelu_kernel(x_ref, w_ref, o_ref):
    acc = jnp.dot(x_ref[...], w_ref[...].astype(BF16), preferred_element_type=F32)
    o_ref[...] = (0.5 * acc * (1.0 + lax.erf(acc * np.float32(np.sqrt(0.5))))).astype(BF16)


def _proj_rope_kernel(x_ref, w_ref, cos_ref, sin_ref, o_ref, *, heads_per_tile):
    acc = jnp.dot(x_ref[...], w_ref[...].astype(BF16), preferred_element_type=F32)
    cs = cos_ref[...]
    sn = sin_ref[...]
    for hh in range(heads_per_tile):
        t = acc[:, hh * HEAD_DIM:(hh + 1) * HEAD_DIM]
        rot = pltpu.roll(t, HEAD_DIM // 2, 1)
        o_ref[:, hh * HEAD_DIM:(hh + 1) * HEAD_DIM] = (t * cs + rot * sn).astype(BF16)


def _proj(hx, w_in, lead, col0, n, rope=None, tm=1024, tn=512):
    r, d = hx.shape
    assert col0 % tn == 0 and n % tn == 0
    in_specs = [pl.BlockSpec((tm, d), lambda i, j: (i, 0)),
                pl.BlockSpec((None,) * len(lead) + (d, tn), lambda i, j: lead + (0, j + col0 // tn))]
    args = [hx, w_in]
    if rope is None:
        body, name = _proj_gelu_kernel, "in_proj_gelu"
    else:
        cosf, sinf, seq = rope
        tiles_per_seq = seq // tm
        tab = pl.BlockSpec((tm, HEAD_DIM), lambda i, j: (i % tiles_per_seq, 0))
        in_specs += [tab, tab]
        args += [cosf, sinf]
        body, name = functools.partial(_proj_rope_kernel, heads_per_tile=tn // HEAD_DIM), "in_proj_rope"
    return pl.pallas_call(
        body,
        grid=(r // tm, n // tn),
        in_specs=in_specs,
        out_specs=pl.BlockSpec((tm, tn), lambda i, j: (i, j)),
        out_shape=jax.ShapeDtypeStruct((r, n), BF16),
        compiler_params=_cparams(("arbitrary", "arbitrary")),
        name=name,
    )(*args)


def _mixer_out_kernel(sink_ref, gu_ref, gv_ref, q_ref, kp_ref, kc_ref, kn_ref, vp_ref, vc_ref, vn_ref,
                      kctx_ref, vctx_ref, ws_ref, bs_ref, gmix_ref, band_ref, edge_ref,
                      wout_ref, res_ref, mu_in_ref, rstd_in_ref, gain_in_ref, bias_in_ref, gate_ref,
                      gain_ref, bias_ref, shift_ref, scale_ref,
                      z_ref, mu_ref, rstd_ref, hm_ref, ybuf, ymix,
                      *, n_groups, n_kv, q_per_kv, seq, n_blocks, alpha):
    t = pl.program_id(0)
    blk = jnp.minimum(t, n_blocks - 1) % (seq // BLOCK)
    slot = t % 2
    mlp_w = n_groups * HEAD_DIM
    attn_w = n_kv * q_per_kv * HEAD_DIM

    @pl.when(t == 0)
    def _():
        ymix[1] = jnp.zeros(ymix.shape[1:], ymix.dtype)

    prev_mix = ymix[1 - slot]
    chunk_w = wout_ref.shape[1] // OUT_PROJ_CHUNKS

    def stage_b_chunk(c):
        cols = slice(c * chunk_w, (c + 1) * chunk_w)
        mix = jnp.dot(prev_mix, wout_ref[:, cols], preferred_element_type=F32)
        h = ((res_ref[:, cols] - mu_in_ref[...]) * rstd_in_ref[...] * gain_in_ref[:, cols]
             + bias_in_ref[:, cols])
        z_ref[:, cols] = alpha * h + gate_ref[:, cols] * mix

    def stage_b_stats():
        mul = gain_ref[...] * (1.0 + scale_ref[...])
        add = bias_ref[...] * (1.0 + scale_ref[...]) + shift_ref[...]
        for r0 in range(0, BLOCK, LN_ROW_CHUNK):
            rows = slice(r0, r0 + LN_ROW_CHUNK)
            z = z_ref[rows, :]
            mu = jnp.mean(z, axis=-1, keepdims=True)
            zc = z - mu
            var = jnp.mean(zc * zc, axis=-1, keepdims=True)
            rstd = lax.rsqrt(var + EPS)
            mu_ref[rows, :] = mu
            rstd_ref[rows, :] = rstd
            hm_ref[rows, :] = (zc * rstd * mul + add).astype(BF16)

    stage_b_chunk(0)

    gv = gv_ref[...].astype(F32)
    mu = jnp.mean(gv, axis=-1, keepdims=True)
    vc_ = gv - mu
    var = jnp.mean(vc_ * vc_, axis=-1, keepdims=True)
    vstd = (vc_ * lax.rsqrt(var + EPS)).astype(BF16)
    ss_mlp = jnp.zeros((CHUNK, 1), F32)
    for g in range(n_groups):
        sl = slice(g * HEAD_DIM, (g + 1) * HEAD_DIM)
        mixed = jnp.dot(ws_ref[g], vstd[:, sl], preferred_element_type=F32) + bs_ref[:, g:g + 1]
        ym = gu_ref[:, sl].astype(F32) * mixed
        ss_mlp = ss_mlp + jnp.sum(ym * ym, axis=-1, keepdims=True)
        ybuf[:, sl] = ym

    stage_b_chunk(1)

    edge = (jnp.where(blk == 0, edge_ref[0:1, :], 0.0)
            + jnp.where(blk == seq // BLOCK - 1, edge_ref[1:2, :], 0.0))
    bias = band_ref[...] + edge
    log2e = np.float32(np.log2(np.e))
    scale2 = np.float32(HEAD_DIM ** -0.5) * log2e
    head = lambda ref, qh: ref[:, qh * HEAD_DIM:(qh + 1) * HEAD_DIM]
    n_keys = band_ref.shape[1]
    ones = jnp.ones((n_keys, HEAD_DIM), BF16)
    ss_attn = jnp.zeros((BLOCK, 1), F32)
    next_chunk = 2
    for h0 in range(0, n_kv, ATTN_HEAD_GROUP):
        heads = range(h0, h0 + ATTN_HEAD_GROUP)
        q_all = jnp.stack([jnp.concatenate([head(q_ref, h * q_per_kv + g) for g in range(q_per_kv)], axis=0)
                           for h in heads])
        k_all = jnp.stack([jnp.concatenate([head(r, h) for r in (kctx_ref, kp_ref, kc_ref, kn_ref)], axis=0)
                           for h in heads])
        v_all = jnp.stack([jnp.concatenate(
            [jnp.concatenate([head(r, h) for r in (vctx_ref, vp_ref, vc_ref, vn_ref)], axis=0), ones], axis=1)
            for h in heads])
        sink2 = jnp.stack([jnp.concatenate([jnp.full((BLOCK, 1), sink_ref[h * q_per_kv + g] * log2e, F32)
                                            for g in range(q_per_kv)], axis=0) for h in heads])
        s = jnp.einsum('hqd,hkd->hqk', q_all, k_all, preferred_element_type=F32) * scale2 + bias[None]
        m = jnp.maximum(jnp.max(s, axis=-1, keepdims=True), sink2)
        stage_b_chunk(next_chunk)
        p = jnp.exp2(s - m)
        pv = jnp.einsum('hqk,hkd->hqd', p.astype(BF16), v_all, preferred_element_type=F32)
        stage_b_chunk(next_chunk + 1)
        next_chunk += 2
        denom = pv[:, :, HEAD_DIM:HEAD_DIM + 1] + jnp.exp2(sink2 - m)
        o = pv[:, :, :HEAD_DIM] * (1.0 / denom)
        for i, h in enumerate(heads):
            for g in range(q_per_kv):
                qh = h * q_per_kv + g
                og = o[i, g * BLOCK:(g + 1) * BLOCK, :]
                ss_attn = ss_attn + jnp.sum(og * og, axis=-1, keepdims=True)
                ybuf[:, mlp_w + qh * HEAD_DIM:mlp_w + (qh + 1) * HEAD_DIM] = og

    stage_b_chunk(6)

    rs_mlp = lax.rsqrt(ss_mlp * np.float32(1.0 / mlp_w) + EPS)
    rs_attn = lax.rsqrt(ss_attn * np.float32(1.0 / attn_w) + EPS)
    ymix[slot, :, :mlp_w] = (ybuf[:, :mlp_w] * rs_mlp * gmix_ref[:, :mlp_w]).astype(BF16)
    ymix[slot, :, mlp_w:] = (ybuf[:, mlp_w:] * rs_attn * gmix_ref[:, mlp_w:]).astype(BF16)

    stage_b_chunk(7)
    stage_b_stats()


def _band_masks(q_per_kv, c_len):
    n_keys = c_len + 3 * BLOCK
    r = np.arange(q_per_kv * BLOCK)[:, None] % BLOCK
    col = np.arange(n_keys)[None, :]
    k_off = col - (c_len + BLOCK)
    in_band = (col < c_len) | (np.abs(k_off - r) <= WINDOW)
    band = np.where(in_band, 0.0, -np.inf).astype(np.float32)
    is_prev = (col >= c_len) & (col < c_len + BLOCK)
    is_next = col >= c_len + 2 * BLOCK
    edge = np.where(np.concatenate([is_prev, is_next], axis=0), -np.inf, 0.0).astype(np.float32)
    return jnp.asarray(band), jnp.asarray(edge)


def _mixer_out(gg, qk, vv, kvctx, w_s, b_sT, sink, g_mix, w_out, res, norm_in, gate, ln, next_mod, alpha,
               batch, seq, n_groups, n_kv, q_per_kv, c_len):
    nb = seq // BLOCK
    n_blocks = batch * nb
    mlp_w = n_groups * HEAD_DIM
    attn_w = n_kv * q_per_kv * HEAD_DIM
    d = w_out.shape[-1]
    kv_w = n_kv * HEAD_DIM
    n_keys = c_len + 3 * BLOCK
    band, edge = _band_masks(q_per_kv, c_len)
    mu_in, rstd_in, gain_in, bias_in = norm_in
    cur = lambda t: jnp.minimum(t, n_blocks - 1)
    prev = lambda t: cur(t) - jnp.where(cur(t) % nb == 0, 0, 1)
    nxt = lambda t: cur(t) + jnp.where(cur(t) % nb == nb - 1, 0, 1)
    lag = lambda t: jnp.maximum(t - 1, 0)
    wide = lambda c: pl.BlockSpec((BLOCK, mlp_w), lambda t: (cur(t), c))
    kvs = lambda rfn, c: pl.BlockSpec((BLOCK, kv_w), lambda t: (rfn(t), c))
    k_col = attn_w // kv_w
    const = lambda shape: pl.BlockSpec(shape, lambda t: (0,) * len(shape), pipeline_mode=pl.Buffered(1))
    lag_row =pl.BlockSpec((BLOCK, d), lambda t: (lag(t), 0))
    lag_stat = pl.BlockSpec((BLOCK, 1), lambda t: (lag(t), 0))
    lag_vec = pl.BlockSpec((None, 1, d), lambda t: (lag(t) // nb, 0, 0))
    return pl.pallas_call(
        functools.partial(_mixer_out_kernel, n_groups=n_groups, n_kv=n_kv, q_per_kv=q_per_kv, seq=seq,
                          n_blocks=n_blocks, alpha=alpha),
        grid=(n_blocks + 1,),
        in_specs=[pl.BlockSpec(memory_space=pltpu.SMEM),
                  wide(0), wide(1), pl.BlockSpec((BLOCK, attn_w), lambda t: (cur(t), 0)),
                  kvs(prev, k_col), kvs(cur, k_col), kvs(nxt, k_col),
                  kvs(prev, 0), kvs(cur, 0), kvs(nxt, 0),
                  pl.BlockSpec((c_len, kv_w), lambda t: (cur(t) // nb, 0)),
                  pl.BlockSpec((c_len, kv_w), lambda t: (cur(t) // nb, 1)),
                  const((n_groups, CHUNK, CHUNK)), const((CHUNK, n_groups)), const((1, mlp_w + attn_w)),
                  const((q_per_kv * BLOCK, n_keys)), const((2, n_keys)),
                  const((mlp_w + attn_w, d)),
                  lag_row, lag_stat, lag_stat, const((1, d)), const((1, d)), lag_vec,
                  const((1, d)), const((1, d)), lag_vec, lag_vec],
        out_specs=[lag_row, lag_stat, lag_stat, lag_row],
        out_shape=[jax.ShapeDtypeStruct((n_blocks * BLOCK, d), F32),
                   jax.ShapeDtypeStruct((n_blocks * BLOCK, 1), F32),
                   jax.ShapeDtypeStruct((n_blocks * BLOCK, 1), F32),
                   jax.ShapeDtypeStruct((n_blocks * BLOCK, d), BF16)],
        scratch_shapes=[pltpu.VMEM((BLOCK, mlp_w + attn_w), F32),
                        pltpu.VMEM((2, BLOCK, mlp_w + attn_w), BF16)],
        compiler_params=_cparams(("arbitrary",), LARGE_VMEM_LIMIT_BYTES),
        name="mixer_out",
    )(sink, gg, gg, qk, qk, qk, qk, vv, vv, vv, kvctx, kvctx, w_s, b_sT, g_mix, band, edge,
      w_out, res, mu_in, rstd_in, gain_in.reshape(1, d), bias_in.reshape(1, d), gate,
      ln[0].reshape(1, d), ln[1].reshape(1, d), next_mod[0], next_mod[1])


def _rope_tables(n):
    rows = n // GRID_W
    row = jnp.broadcast_to(jnp.arange(rows, dtype=F32)[:, None], (rows, GRID_W)).reshape(n)
    col = jnp.broadcast_to(jnp.arange(GRID_W, dtype=F32)[None, :], (rows, GRID_W)).reshape(n)
    n_freq = HEAD_DIM // 4
    inv_freq = ROPE_BASE ** (-jnp.arange(n_freq, dtype=F32) / n_freq)
    ang = jnp.concatenate([row[:, None] * inv_freq, col[:, None] * inv_freq], axis=-1)
    cs, sn = jnp.cos(ang), jnp.sin(ang)
    return jnp.concatenate([cs, cs], axis=-1), jnp.concatenate([-sn, sn], axis=-1)


def kernel(x, c, ctx, c_ctx, w_ada, b_ada, w_ffn_gate, w_ffn_up, w_ffn_down, w_in, w_spatial, b_spatial,
           sink_logit, g_mix, w_out, ln_gain, ln_bias):
    b, n, d = x.shape
    c_len = ctx.shape[1]
    depth = w_ada.shape[0]
    d_ff = w_ffn_gate.shape[-1]
    n_groups = w_spatial.shape[1]
    mlp_w = n_groups * HEAD_DIM
    n_q = sink_logit.shape[1]
    attn_w = n_q * HEAD_DIM
    kv_w = (w_in.shape[-1] - 2 * mlp_w - attn_w) // 2
    n_kv = kv_w // HEAD_DIM
    q_per_kv = n_q // n_kv
    assert depth == 1 and b + 1 <= 8
    alpha = float((2.0 * depth) ** 0.25)
    ctx_seg = b

    x_seg = lambda r0: r0 // n
    c_seg = lambda r0: ctx_seg

    cosf, sinf = _rope_tables(n)
    x2 = x.reshape(b * n, d)
    ctx2 = ctx.reshape(b * c_len, d)

    layer = 0
    c8 = jnp.concatenate([c, c_ctx[None, :], jnp.zeros((8 - b - 1, d), F32)], axis=0)
    m = _ada(c8, w_ada[layer], b_ada[layer]).reshape(8, N_SUB, N_MOD, d)
    mod = lambda s, k: m[:, s, k, :][:, None, :]

    def ffn_down_z(a, w_down_b, res, idx, s, seg_fn, norm=None):
        return _matmul_resid(a, w_down_b, (layer, idx), res, mod(s, 2), seg_fn, alpha, 0.5,
                             tm=512, tn=512, tk=d_ff, name="ffn_down", norm=norm)

    xm = _modcast(x2, mod(0, 0), mod(0, 1), x_seg)
    cm = _modcast(ctx2, mod(0, 0), mod(0, 1), c_seg)
    mix_mod = (mod(1, 0), mod(1, 1))
    ln0 = (ln_gain[layer, 0], ln_bias[layer, 0])
    a_x, (w_down_b, w_out_b) = _swiglu_up(xm, w_ffn_gate, w_ffn_up, (layer, 0), tm=UP_TM, tn=FF_TILE,
                                          side_casts=(w_ffn_down.reshape(-1, d), w_out[layer]))
    w_down_b = w_down_b.reshape(w_ffn_down.shape)
    a_c, _ = _swiglu_up(cm, w_ffn_gate, w_ffn_up, (layer, 0), tm=1024, tn=FF_TILE)
    z1 = ffn_down_z(a_x, w_down_b, x2, 0, 0, x_seg)
    z1c = ffn_down_z(a_c, w_down_b, ctx2, 0, 0, c_seg)
    mu1, rstd1, hx = _ln(z1, *ln0, seg_fn=x_seg, next_mod=mix_mod)
    _, _, hc = _ln(z1c, *ln0, seg_fn=c_seg, next_mod=mix_mod)

    gg = _proj(hx, w_in, (layer,), 0, 2 * mlp_w)
    qk = _proj(hx, w_in, (layer,), 2 * mlp_w, attn_w + kv_w, rope=(cosf, sinf, n))
    vv = _matmul(hx, w_in, (layer,), BF16, tm=1024, tn=512, tk=d, name="in_proj_v",
                 col0=2 * mlp_w + attn_w + kv_w, n=kv_w)
    kvctx = _matmul(hc, w_in, (layer,), BF16, tm=b * c_len, tn=kv_w, tk=d, name="ctx_kv",
                    col0=2 * mlp_w + attn_w)
    ln1 = (ln_gain[layer, 1], ln_bias[layer, 1])
    z2, mu2, rstd2, hm2 = _mixer_out(
        gg, qk, vv, kvctx, w_spatial[layer].astype(BF16), b_spatial[layer].T, sink_logit[layer],
        g_mix[layer].reshape(1, mlp_w + attn_w), w_out_b, z1, (mu1, rstd1) + ln0, mod(1, 2),
        ln1, (mod(2, 0), mod(2, 1)), alpha, b, n, n_groups, n_kv, q_per_kv, c_len)

    a_2, _ = _swiglu_up(hm2, w_ffn_gate, w_ffn_up, (layer, 1), tm=UP_TM, tn=FF_TILE)
    z3 = ffn_down_z(a_2, w_down_b, z2, 1, 2, x_seg, norm=(mu2, rstd2) + ln1)
    out = _ln(z3, ln_gain[layer, 2], ln_bias[layer, 2])
    return out.reshape(b, n, d)
```

```python
import functools

import jax
import jax.numpy as jnp
import numpy as np
from jax import lax
from jax.experimental import pallas as pl
from jax.experimental.pallas import tpu as pltpu

F32 = jnp.float32
BF16 = jnp.bfloat16

HEAD_DIM = 128
CHUNK = 128
WINDOW = 128
BLOCK = 128
GRID_W = 64
ROPE_BASE = 10000.0
EPS = 1e-6
N_SUB = 3
N_MOD = 3

VMEM_LIMIT_BYTES = 58 * 1024 * 1024
LARGE_VMEM_LIMIT_BYTES = 62 * 1024 * 1024
FF_TILE = 256
UP_TM = 2048
PROJ_ROW_CHUNKS = 4
ATTN_HEAD_GROUP = 2
OUT_PROJ_CHUNKS = 8
LN_ROW_CHUNK = 16


def _cparams(sem, vmem_limit_bytes=VMEM_LIMIT_BYTES):
    return pltpu.CompilerParams(dimension_semantics=sem, vmem_limit_bytes=vmem_limit_bytes)


def _ada_kernel(c_ref, w_ref, b_ref, o_ref):
    c = c_ref[...]
    s = (c * jax.nn.sigmoid(c)).astype(BF16)
    o_ref[...] = jnp.dot(s, w_ref[...].astype(BF16), preferred_element_type=F32) + b_ref[...]


def _ada(c8, w_ada, b_ada, tn=512):
    d, n = w_ada.shape
    return pl.pallas_call(
        _ada_kernel,
        grid=(n // tn,),
        in_specs=[pl.BlockSpec((8, d), lambda j: (0, 0)),
                  pl.BlockSpec((d, tn), lambda j: (0, j)),
                  pl.BlockSpec((1, tn), lambda j: (0, j))],
        out_specs=pl.BlockSpec((8, tn), lambda j: (0, j)),
        out_shape=jax.ShapeDtypeStruct((8, n), F32),
        compiler_params=_cparams(("arbitrary",)),
        name="ada",
    )(c8, w_ada, b_ada.reshape(1, n))


def _modcast_kernel(x_ref, shift_ref, scale_ref, o_ref):
    o_ref[...] = (x_ref[...] * (1.0 + scale_ref[...]) + shift_ref[...]).astype(BF16)


def _modcast(x2d, shift, scale, seg_fn, tm=256):
    r, d = x2d.shape
    vec = pl.BlockSpec((None, 1, d), lambda i: (seg_fn(i * tm), 0, 0))
    return pl.pallas_call(
        _modcast_kernel,
        grid=(r // tm,),
        in_specs=[pl.BlockSpec((tm, d), lambda i: (i, 0)), vec, vec],
        out_specs=pl.BlockSpec((tm, d), lambda i: (i, 0)),
        out_shape=jax.ShapeDtypeStruct((r, d), BF16),
        compiler_params=_cparams(("arbitrary",)),
        name="modcast",
    )(x2d, shift, scale)


def _swiglu_up_kernel(x_ref, wg_ref, wu_ref, *rest):
    o_ref = rest[len(rest) // 2]
    x = x_ref[...]
    g = jnp.dot(x, wg_ref[...].astype(BF16), preferred_element_type=F32)
    u = jnp.dot(x, wu_ref[...].astype(BF16), preferred_element_type=F32)
    o_ref[...] = (g * jax.nn.sigmoid(g) * u).astype(BF16)
    n_side = len(rest) // 2
    for src_ref, dst_ref in zip(rest[:n_side], rest[n_side + 1:]):
        dst_ref[...] = src_ref[...].astype(BF16)


def _swiglu_up(xm, w_gate, w_up, lead, tm, tn, side_casts=()):
    r, d = xm.shape
    f = w_gate.shape[-1]
    ni, nj = r // tm, f // tn
    wspec = pl.BlockSpec((None,) * len(lead) + (d, tn), lambda i, j: lead + (0, j))
    side_specs, side_shapes = [], []
    for w in side_casts:
        rows = -(-w.shape[0] // (ni * nj))
        rows = -(-rows // 16) * 16
        n_slabs = w.shape[0] // rows
        assert n_slabs * rows == w.shape[0] and n_slabs <= ni * nj
        side_specs.append(pl.BlockSpec((rows, w.shape[1]),
                                       lambda i, j, n_slabs=n_slabs: (jnp.minimum(i * nj + j, n_slabs - 1), 0)))
        side_shapes.append(jax.ShapeDtypeStruct(w.shape, BF16))
    outs = pl.pallas_call(
        _swiglu_up_kernel,
        grid=(ni, nj),
        in_specs=[pl.BlockSpec((tm, d), lambda i, j: (i, 0)), wspec, wspec] + side_specs,
        out_specs=[pl.BlockSpec((tm, tn), lambda i, j: (i, j))] + side_specs,
        out_shape=[jax.ShapeDtypeStruct((r, f), BF16)] + side_shapes,
        compiler_params=_cparams(("arbitrary", "arbitrary"), LARGE_VMEM_LIMIT_BYTES),
        name="swiglu_up",
    )(xm, w_gate, w_up, *side_casts)
    return outs[0], outs[1:]


def _mm_kernel(a_ref, b_ref, o_ref, *, nk):
    d = jnp.dot(a_ref[...], b_ref[...].astype(BF16), preferred_element_type=F32)
    if nk == 1:
        o_ref[...] = d.astype(o_ref.dtype)
    else:
        k = pl.program_id(2)

        @pl.when(k == 0)
        def _():
            o_ref[...] = d

        @pl.when(k > 0)
        def _():
            o_ref[...] += d


def _matmul(a, b, lead, out_dtype, tm, tn, tk, name, col0=0, n=None):
    r, kk = a.shape
    n = b.shape[-1] - col0 if n is None else n
    nk = kk // tk
    assert nk * tk == kk and col0 % tn == 0 and n % tn == 0
    assert nk == 1 or out_dtype == F32
    return pl.pallas_call(
        functools.partial(_mm_kernel, nk=nk),
        grid=(r // tm, n // tn, nk),
        in_specs=[pl.BlockSpec((tm, tk), lambda i, j, k: (i, k)),
                  pl.BlockSpec((None,) * len(lead) + (tk, tn), lambda i, j, k: lead + (k, j + col0 // tn))],
        out_specs=pl.BlockSpec((tm, tn), lambda i, j, k: (i, j)),
        out_shape=jax.ShapeDtypeStruct((r, n), out_dtype),
        compiler_params=_cparams(("arbitrary", "arbitrary", "arbitrary")),
        name=name,
    )(a, b)


def _mm_resid_kernel(*refs, nk, alpha, coef, normed):
    if normed:
        a_ref, b_ref, res_ref, gate_ref, mu_ref, rstd_ref, gain_ref, bias_ref, o_ref = refs
    else:
        a_ref, b_ref, res_ref, gate_ref, o_ref = refs
    def partial_product():
        return jnp.dot(a_ref[...], b_ref[...].astype(BF16), preferred_element_type=F32)

    def finish(y):
        h = res_ref[...]
        if normed:
            h = (h - mu_ref[...]) * rstd_ref[...] * gain_ref[...] + bias_ref[...]
        o_ref[...] = alpha * h + (coef * gate_ref[...]) * y

    if nk == 1:
        finish(partial_product())
    else:
        k = pl.program_id(2)

        @pl.when(k == 0)
        def _():
            o_ref[...] = partial_product()

        if nk > 2:
            @pl.when((k > 0) & (k < nk - 1))
            def _():
                o_ref[...] += partial_product()

        @pl.when(k == nk - 1)
        def _():
            finish(o_ref[...] + partial_product())


def _matmul_resid(a, b, lead, res, gate, seg_fn, alpha, coef, tm, tn, tk, name, norm=None):
    r, kk = a.shape
    n = b.shape[-1]
    nk = kk // tk
    assert nk * tk == kk and nk >= 1 and n % tn == 0
    tile = pl.BlockSpec((tm, tn), lambda i, j, k: (i, j))
    vec = pl.BlockSpec((None, 1, tn), lambda i, j, k: (seg_fn(i * tm), 0, j))
    in_specs = [pl.BlockSpec((tm, tk), lambda i, j, k: (i, k)),
                pl.BlockSpec((None,) * len(lead) + (tk, tn), lambda i, j, k: lead + (k, j)),
                tile, vec]
    args = [a, b, res, gate]
    if norm is not None:
        mu, rstd, gain, bias = norm
        stat = pl.BlockSpec((tm, 1), lambda i, j, k: (i, 0))
        par = pl.BlockSpec((1, tn), lambda i, j, k: (0, j))
        in_specs += [stat, stat, par, par]
        args += [mu, rstd, gain.reshape(1, n), bias.reshape(1, n)]
    return pl.pallas_call(
        functools.partial(_mm_resid_kernel, nk=nk, alpha=alpha, coef=coef, normed=norm is not None),
        grid=(r // tm, n // tn, nk),
        in_specs=in_specs,
        out_specs=tile,
        out_shape=jax.ShapeDtypeStruct((r, n), F32),
        compiler_params=_cparams(("arbitrary", "arbitrary", "arbitrary")),
        name=name,
    )(*args)


def _ln_kernel(*refs, final):
    if final:
        z_ref, gain_ref, bias_ref, o_ref = refs
    else:
        z_ref, gain_ref, bias_ref, shift_ref, scale_ref, mu_ref, rstd_ref, om_ref = refs
    if final:
        mul, add = gain_ref[...], bias_ref[...]
    else:
        mul = gain_ref[...] * (1.0 + scale_ref[...])
        add = bias_ref[...] * (1.0 + scale_ref[...]) + shift_ref[...]
    for r0 in range(0, z_ref.shape[0], LN_ROW_CHUNK):
        rows = slice(r0, r0 + LN_ROW_CHUNK)
        z = z_ref[rows, :]
        mu = jnp.mean(z, axis=-1, keepdims=True)
        zc = z - mu
        var = jnp.mean(zc * zc, axis=-1, keepdims=True)
        rstd = lax.rsqrt(var + EPS)
        y = zc * rstd * mul + add
        if final:
            o_ref[rows, :] = y
        else:
            mu_ref[rows, :] = mu
            rstd_ref[rows, :] = rstd
            om_ref[rows, :] = y.astype(BF16)


def _ln(z, gain, bias, seg_fn=None, next_mod=None, tm=256):
    r, d = z.shape
    row = pl.BlockSpec((tm, d), lambda i: (i, 0))
    full = pl.BlockSpec((1, d), lambda i: (0, 0))
    final = next_mod is None
    in_specs = [row, full, full]
    args = [z, gain.reshape(1, d), bias.reshape(1, d)]
    if final:
        out_specs = row
        out_shape = jax.ShapeDtypeStruct((r, d), F32)
    else:
        vec = pl.BlockSpec((None, 1, d), lambda i: (seg_fn(i * tm), 0, 0))
        stat = pl.BlockSpec((tm, 1), lambda i: (i, 0))
        in_specs += [vec, vec]
        args += list(next_mod)
        out_specs = [stat, stat, row]
        out_shape = [jax.ShapeDtypeStruct((r, 1), F32), jax.ShapeDtypeStruct((r, 1), F32),
                     jax.ShapeDtypeStruct((r, d), BF16)]
    return pl.pallas_call(
        functools.partial(_ln_kernel, final=final),
        grid=(r // tm,),
        in_specs=in_specs,
        out_specs=out_specs,
        out_shape=out_shape,
        compiler_params=_cparams(("arbitrary",)),
        name="ln_final" if final else "ln_stats",
    )(*args)


def _proj_gelu_kernel(x_ref, w_ref, o_ref):
    acc = jnp.dot(x_ref[...], w_ref[...].astype(BF16), preferred_element_type=F32)
    o_ref[...] = (0.5 * acc * (1.0 + lax.erf(acc * np.float32(np.sqrt(0.5))))).astype(BF16)


def _ln_proj_gelu_kernel(z_ref, gain_ref, bias_ref, shift_ref, scale_ref, w_ref,
                         mu_ref, rstd_ref, hm_ref, o_ref, hx_even, hx_odd, *, slab):
    i = pl.program_id(0)
    j = pl.program_id(1)

    def normalise_slab(hx, first=0, last=None):
        mul = gain_ref[...] * (1.0 + scale_ref[...])
        add = bias_ref[...] * (1.0 + scale_ref[...]) + shift_ref[...]
        for r0 in range(first, slab if last is None else last, LN_ROW_CHUNK):
            rows = slice(r0, r0 + LN_ROW_CHUNK)
            z = z_ref[rows, :]
            mu = jnp.mean(z, axis=-1, keepdims=True)
            zc = z - mu
            var = jnp.mean(zc * zc, axis=-1, keepdims=True)
            rstd = lax.rsqrt(var + EPS)
            mu_ref[rows, :] = mu
            rstd_ref[rows, :] = rstd
            y = (zc * rstd * mul + add).astype(BF16)
            hm_ref[rows, :] = y
            hx[pl.ds(pl.multiple_of(j * slab, slab) + r0, LN_ROW_CHUNK), :] = y

    def project_and_normalise(hx_prev, hx_next):
        w = w_ref[...].astype(BF16)
        tm = o_ref.shape[0]
        for c in range(PROJ_ROW_CHUNKS):
            rows = slice(c * tm // PROJ_ROW_CHUNKS, (c + 1) * tm // PROJ_ROW_CHUNKS)
            acc = jnp.dot(hx_prev[rows, :], w, preferred_element_type=F32)
            o_ref[rows, :] = (0.5 * acc * (1.0 + lax.erf(acc * np.float32(np.sqrt(0.5))))).astype(BF16)
            normalise_slab(hx_next, c * slab // PROJ_ROW_CHUNKS, (c + 1) * slab // PROJ_ROW_CHUNKS)

    @pl.when(i == 0)
    def _():
        normalise_slab(hx_even)

    @pl.when((i > 0) & (i % 2 == 0))
    def _():
        project_and_normalise(hx_odd, hx_even)

    @pl.when(i % 2 == 1)
    def _():
        project_and_normalise(hx_even, hx_odd)


def _ln_proj_gelu(z, gain, bias, seg_fn, next_mod, w_in, lead, n, tm=1024, tn=512):
    r, d = z.shape
    n_i, n_j = r // tm, n // tn
    slab = tm // n_j
    assert slab * n_j == tm and slab % LN_ROW_CHUNK == 0
    ln_tile = lambda i: jnp.minimum(i, n_i - 1)
    slab_spec = pl.BlockSpec((slab, d), lambda i, j: (ln_tile(i) * n_j + j, 0))
    stat_spec = pl.BlockSpec((slab, 1), lambda i, j: (ln_tile(i) * n_j + j, 0))
    vec = pl.BlockSpec((None, 1, d), lambda i, j: (seg_fn(ln_tile(i) * tm), 0, 0))
    full = pl.BlockSpec((1, d), lambda i, j: (0, 0))
    return pl.pallas_call(
        functools.partial(_ln_proj_gelu_kernel, slab=slab),
        grid=(n_i + 1, n_j),
        in_specs=[slab_spec, full, full, vec, vec,
                  pl.BlockSpec((None,) * len(lead) + (d, tn), lambda i, j: lead + (0, j))],
        out_specs=[stat_spec, stat_spec, slab_spec,
                   pl.BlockSpec((tm, tn), lambda i, j: (jnp.maximum(i - 1, 0), j))],
        out_shape=[jax.ShapeDtypeStruct((r, 1), F32), jax.ShapeDtypeStruct((r, 1), F32),
                   jax.ShapeDtypeStruct((r, d), BF16), jax.ShapeDtypeStruct((r, n), BF16)],
        scratch_shapes=[pltpu.VMEM((tm, d), BF16), pltpu.VMEM((tm, d), BF16)],
        compiler_params=_cparams(("arbitrary", "arbitrary")),
        name="ln_proj_gelu",
    )(z, gain.reshape(1, d), bias.reshape(1, d), next_mod[0], next_mod[1], w_in)


def _proj_rope_kernel(x_ref, w_ref, cos_ref, sin_ref, o_ref, *, heads_per_tile):
    acc = jnp.dot(x_ref[...], w_ref[...].astype(BF16), preferred_element_type=F32)
    cs = cos_ref[...]
    sn = sin_ref[...]
    for hh in range(heads_per_tile):
        t = acc[:, hh * HEAD_DIM:(hh + 1) * HEAD_DIM]
        rot = pltpu.roll(t, HEAD_DIM // 2, 1)
        o_ref[:, hh * HEAD_DIM:(hh + 1) * HEAD_DIM] = (t * cs + rot * sn).astype(BF16)


def _proj(hx, w_in, lead, col0, n, rope=None, tm=1024, tn=512):
    r, d = hx.shape
    assert col0 % tn == 0 and n % tn == 0
    in_specs = [pl.BlockSpec((tm, d), lambda i, j: (i, 0)),
                pl.BlockSpec((None,) * len(lead) + (d, tn), lambda i, j: lead + (0, j + col0 // tn))]
    args = [hx, w_in]
    if rope is None:
        body, name = _proj_gelu_kernel, "in_proj_gelu"
    else:
        cosf, sinf, seq = rope
        tiles_per_seq = seq // tm
        tab = pl.BlockSpec((tm, HEAD_DIM), lambda i, j: (i % tiles_per_seq, 0))
        in_specs += [tab, tab]
        args += [cosf, sinf]
        body, name = functools.partial(_proj_rope_kernel, heads_per_tile=tn // HEAD_DIM), "in_proj_rope"
    return pl.pallas_call(
        body,
        grid=(r // tm, n // tn),
        in_specs=in_specs,
        out_specs=pl.BlockSpec((tm, tn), lambda i, j: (i, j)),
        out_shape=jax.ShapeDtypeStruct((r, n), BF16),
        compiler_params=_cparams(("arbitrary", "arbitrary")),
        name=name,
    )(*args)


def _mixer_out_kernel(sink_ref, gu_ref, gv_ref, q_ref, kp_ref, kc_ref, kn_ref, vp_ref, vc_ref, vn_ref,
                      kctx_ref, vctx_ref, ws_ref, bs_ref, gmix_ref, band_ref, edge_ref,
                      wout_ref, res_ref, mu_in_ref, rstd_in_ref, gain_in_ref, bias_in_ref, gate_ref,
                      gain_ref, bias_ref, shift_ref, scale_ref,
                      z_ref, mu_ref, rstd_ref, hm_ref, ybuf, ymix,
                      *, n_groups, n_kv, q_per_kv, seq, n_blocks, alpha):
    t = pl.program_id(0)
    blk = jnp.minimum(t, n_blocks - 1) % (seq // BLOCK)
    slot = t % 2
    mlp_w = n_groups * HEAD_DIM
    attn_w = n_kv * q_per_kv * HEAD_DIM

    @pl.when(t == 0)
    def _():
        ymix[1] = jnp.zeros(ymix.shape[1:], ymix.dtype)

    prev_mix = ymix[1 - slot]
    chunk_w = wout_ref.shape[1] // OUT_PROJ_CHUNKS

    def stage_b_chunk(c):
        cols = slice(c * chunk_w, (c + 1) * chunk_w)
        mix = jnp.dot(prev_mix, wout_ref[:, cols], preferred_element_type=F32)
        h = ((res_ref[:, cols] - mu_in_ref[...]) * rstd_in_ref[...] * gain_in_ref[:, cols]
             + bias_in_ref[:, cols])
        z_ref[:, cols] = alpha * h + gate_ref[:, cols] * mix

    def stage_b_stats():
        mul = gain_ref[...] * (1.0 + scale_ref[...])
        add = bias_ref[...] * (1.0 + scale_ref[...]) + shift_ref[...]
        for r0 in range(0, BLOCK, LN_ROW_CHUNK):
            rows = slice(r0, r0 + LN_ROW_CHUNK)
            z = z_ref[rows, :]
            mu = jnp.mean(z, axis=-1, keepdims=True)
            zc = z - mu
            var = jnp.mean(zc * zc, axis=-1, keepdims=True)
            rstd = lax.rsqrt(var + EPS)
            mu_ref[rows, :] = mu
            rstd_ref[rows, :] = rstd
            hm_ref[rows, :] = (zc * rstd * mul + add).astype(BF16)

    stage_b_chunk(0)

    gv = gv_ref[...].astype(F32)
    mu = jnp.mean(gv, axis=-1, keepdims=True)
    vc_ = gv - mu
    var = jnp.mean(vc_ * vc_, axis=-1, keepdims=True)
    vstd = (vc_ * lax.rsqrt(var + EPS)).astype(BF16)
    ss_mlp = jnp.zeros((CHUNK, 1), F32)
    for g in range(n_groups):
        sl = slice(g * HEAD_DIM, (g + 1) * HEAD_DIM)
        mixed = jnp.dot(ws_ref[g], vstd[:, sl], preferred_element_type=F32) + bs_ref[:, g:g + 1]
        ym = gu_ref[:, sl].astype(F32) * mixed
        ss_mlp = ss_mlp + jnp.sum(ym * ym, axis=-1, keepdims=True)
        ybuf[:, sl] = ym

    stage_b_chunk(1)

    edge = (jnp.where(blk == 0, edge_ref[0:1, :], 0.0)
            + jnp.where(blk == seq // BLOCK - 1, edge_ref[1:2, :], 0.0))
    bias = band_ref[...] + edge
    log2e = np.float32(np.log2(np.e))
    scale2 = np.float32(HEAD_DIM ** -0.5) * log2e
    head = lambda ref, qh: ref[:, qh * HEAD_DIM:(qh + 1) * HEAD_DIM]
    n_keys = band_ref.shape[1]
    ones = jnp.ones((n_keys, HEAD_DIM), BF16)
    ss_attn = jnp.zeros((BLOCK, 1), F32)
    next_chunk = 2
    for h0 in range(0, n_kv, ATTN_HEAD_GROUP):
        heads = range(h0, h0 + ATTN_HEAD_GROUP)
        q_all = jnp.stack([jnp.concatenate([head(q_ref, h * q_per_kv + g) for g in range(q_per_kv)], axis=0)
                           for h in heads])
        k_all = jnp.stack([jnp.concatenate([head(r, h) for r in (kctx_ref, kp_ref, kc_ref, kn_ref)], axis=0)
                           for h in heads])
        v_all = jnp.stack([jnp.concatenate(
            [jnp.concatenate([head(r, h) for r in (vctx_ref, vp_ref, vc_ref, vn_ref)], axis=0), ones], axis=1)
            for h in heads])
        sink2 = jnp.stack([jnp.concatenate([jnp.full((BLOCK, 1), sink_ref[h * q_per_kv + g] * log2e, F32)
                                            for g in range(q_per_kv)], axis=0) for h in heads])
        s = jnp.einsum('hqd,hkd->hqk', q_all, k_all, preferred_element_type=F32) * scale2 + bias[None]
        m = jnp.maximum(jnp.max(s, axis=-1, keepdims=True), sink2)
        stage_b_chunk(next_chunk)
        p = jnp.exp2(s - m)
        pv = jnp.einsum('hqk,hkd->hqd', p.astype(BF16), v_all, preferred_element_type=F32)
        stage_b_chunk(next_chunk + 1)
        next_chunk += 2
        denom = pv[:, :, HEAD_DIM:HEAD_DIM + 1] + jnp.exp2(sink2 - m)
        o = pv[:, :, :HEAD_DIM] * (1.0 / denom)
        for i, h in enumerate(heads):
            for g in range(q_per_kv):
                qh = h * q_per_kv + g
                og = o[i, g * BLOCK:(g + 1) * BLOCK, :]
                ss_attn = ss_attn + jnp.sum(og * og, axis=-1, keepdims=True)
                ybuf[:, mlp_w + qh * HEAD_DIM:mlp_w + (qh + 1) * HEAD_DIM] = og

    stage_b_chunk(6)

    rs_mlp = lax.rsqrt(ss_mlp * np.float32(1.0 / mlp_w) + EPS)
    rs_attn = lax.rsqrt(ss_attn * np.float32(1.0 / attn_w) + EPS)
    ymix[slot, :, :mlp_w] = (ybuf[:, :mlp_w] * rs_mlp * gmix_ref[:, :mlp_w]).astype(BF16)
    ymix[slot, :, mlp_w:] = (ybuf[:, mlp_w:] * rs_attn * gmix_ref[:, mlp_w:]).astype(BF16)

    stage_b_chunk(7)
    stage_b_stats()


def _band_masks(q_per_kv, c_len):
    n_keys = c_len + 3 * BLOCK
    r = np.arange(q_per_kv * BLOCK)[:, None] % BLOCK
    col = np.arange(n_keys)[None, :]
    k_off = col - (c_len + BLOCK)
    in_band = (col < c_len) | (np.abs(k_off - r) <= WINDOW)
    band = np.where(in_band, 0.0, -np.inf).astype(np.float32)
    is_prev = (col >= c_len) & (col < c_len + BLOCK)
    is_next = col >= c_len + 2 * BLOCK
    edge = np.where(np.concatenate([is_prev, is_next], axis=0), -np.inf, 0.0).astype(np.float32)
    return jnp.asarray(band), jnp.asarray(edge)


def _mixer_out(gg, qk, vv, kvctx, w_s, b_sT, sink, g_mix, w_out, res, norm_in, gate, ln, next_mod, alpha,
               batch, seq, n_groups, n_kv, q_per_kv, c_len):
    nb = seq // BLOCK
    n_blocks = batch * nb
    mlp_w = n_groups * HEAD_DIM
    attn_w = n_kv * q_per_kv * HEAD_DIM
    d = w_out.shape[-1]
    kv_w = n_kv * HEAD_DIM
    n_keys = c_len + 3 * BLOCK
    band, edge = _band_masks(q_per_kv, c_len)
    mu_in, rstd_in, gain_in, bias_in = norm_in
    cur = lambda t: jnp.minimum(t, n_blocks - 1)
    prev = lambda t: cur(t) - jnp.where(cur(t) % nb == 0, 0, 1)
    nxt = lambda t: cur(t) + jnp.where(cur(t) % nb == nb - 1, 0, 1)
    lag = lambda t: jnp.maximum(t - 1, 0)
    wide = lambda c: pl.BlockSpec((BLOCK, mlp_w), lambda t: (cur(t), c))
    kvs = lambda rfn, c: pl.BlockSpec((BLOCK, kv_w), lambda t: (rfn(t), c))
    k_col = attn_w // kv_w
    const = lambda shape: pl.BlockSpec(shape, lambda t: (0,) * len(shape), pipeline_mode=pl.Buffered(1))
    lag_row =pl.BlockSpec((BLOCK, d), lambda t: (lag(t), 0))
    lag_stat = pl.BlockSpec((BLOCK, 1), lambda t: (lag(t), 0))
    lag_vec = pl.BlockSpec((None, 1, d), lambda t: (lag(t) // nb, 0, 0))
    return pl.pallas_call(
        functools.partial(_mixer_out_kernel, n_groups=n_groups, n_kv=n_kv, q_per_kv=q_per_kv, seq=seq,
                          n_blocks=n_blocks, alpha=alpha),
        grid=(n_blocks + 1,),
        in_specs=[pl.BlockSpec(memory_space=pltpu.SMEM),
                  wide(0), wide(1), pl.BlockSpec((BLOCK, attn_w), lambda t: (cur(t), 0)),
                  kvs(prev, k_col), kvs(cur, k_col), kvs(nxt, k_col),
                  kvs(prev, 0), kvs(cur, 0), kvs(nxt, 0),
                  pl.BlockSpec((c_len, kv_w), lambda t: (cur(t) // nb, 0)),
                  pl.BlockSpec((c_len, kv_w), lambda t: (cur(t) // nb, 1)),
                  const((n_groups, CHUNK, CHUNK)), const((CHUNK, n_groups)), const((1, mlp_w + attn_w)),
                  const((q_per_kv * BLOCK, n_keys)), const((2, n_keys)),
                  const((mlp_w + attn_w, d)),
                  lag_row, lag_stat, lag_stat, const((1, d)), const((1, d)), lag_vec,
                  const((1, d)), const((1, d)), lag_vec, lag_vec],
        out_specs=[lag_row, lag_stat, lag_stat, lag_row],
        out_shape=[jax.ShapeDtypeStruct((n_blocks * BLOCK, d), F32),
                   jax.ShapeDtypeStruct((n_blocks * BLOCK, 1), F32),
                   jax.ShapeDtypeStruct((n_blocks * BLOCK, 1), F32),
                   jax.ShapeDtypeStruct((n_blocks * BLOCK, d), BF16)],
        scratch_shapes=[pltpu.VMEM((BLOCK, mlp_w + attn_w), F32),
                        pltpu.VMEM((2, BLOCK, mlp_w + attn_w), BF16)],
        compiler_params=_cparams(("arbitrary",), LARGE_VMEM_LIMIT_BYTES),
        name="mixer_out",
    )(sink, gg, gg, qk, qk, qk, qk, vv, vv, vv, kvctx, kvctx, w_s, b_sT, g_mix, band, edge,
      w_out, res, mu_in, rstd_in, gain_in.reshape(1, d), bias_in.reshape(1, d), gate,
      ln[0].reshape(1, d), ln[1].reshape(1, d), next_mod[0], next_mod[1])


def _rope_tables(n):
    rows = n // GRID_W
    row = jnp.broadcast_to(jnp.arange(rows, dtype=F32)[:, None], (rows, GRID_W)).reshape(n)
    col = jnp.broadcast_to(jnp.arange(GRID_W, dtype=F32)[None, :], (rows, GRID_W)).reshape(n)
    n_freq = HEAD_DIM // 4
    inv_freq = ROPE_BASE ** (-jnp.arange(n_freq, dtype=F32) / n_freq)
    ang = jnp.concatenate([row[:, None] * inv_freq, col[:, None] * inv_freq], axis=-1)
    cs, sn = jnp.cos(ang), jnp.sin(ang)
    return jnp.concatenate([cs, cs], axis=-1), jnp.concatenate([-sn, sn], axis=-1)


def kernel(x, c, ctx, c_ctx, w_ada, b_ada, w_ffn_gate, w_ffn_up, w_ffn_down, w_in, w_spatial, b_spatial,
           sink_logit, g_mix, w_out, ln_gain, ln_bias):
    b, n, d = x.shape
    c_len = ctx.shape[1]
    depth = w_ada.shape[0]
    d_ff = w_ffn_gate.shape[-1]
    n_groups = w_spatial.shape[1]
    mlp_w = n_groups * HEAD_DIM
    n_q = sink_logit.shape[1]
    attn_w = n_q * HEAD_DIM
    kv_w = (w_in.shape[-1] - 2 * mlp_w - attn_w) // 2
    n_kv = kv_w // HEAD_DIM
    q_per_kv = n_q // n_kv
    assert depth == 1 and b + 1 <= 8
    alpha = float((2.0 * depth) ** 0.25)
    ctx_seg = b

    x_seg = lambda r0: r0 // n
    c_seg = lambda r0: ctx_seg

    cosf, sinf = _rope_tables(n)
    x2 = x.reshape(b * n, d)
    ctx2 = ctx.reshape(b * c_len, d)

    layer = 0
    c8 = jnp.concatenate([c, c_ctx[None, :], jnp.zeros((8 - b - 1, d), F32)], axis=0)
    m = _ada(c8, w_ada[layer], b_ada[layer]).reshape(8, N_SUB, N_MOD, d)
    mod = lambda s, k: m[:, s, k, :][:, None, :]

    def ffn_down_z(a, w_down_b, res, idx, s, seg_fn, norm=None):
        return _matmul_resid(a, w_down_b, (layer, idx), res, mod(s, 2), seg_fn, alpha, 0.5,
                             tm=512, tn=512, tk=d_ff, name="ffn_down", norm=norm)

    xm = _modcast(x2, mod(0, 0), mod(0, 1), x_seg)
    cm = _modcast(ctx2, mod(0, 0), mod(0, 1), c_seg)
    mix_mod = (mod(1, 0), mod(1, 1))
    ln0 = (ln_gain[layer, 0], ln_bias[layer, 0])
    a_x, (w_down_b, w_out_b) = _swiglu_up(xm, w_ffn_gate, w_ffn_up, (layer, 0), tm=UP_TM, tn=FF_TILE,
                                          side_casts=(w_ffn_down.reshape(-1, d), w_out[layer]))
    w_down_b = w_down_b.reshape(w_ffn_down.shape)
    a_c, _ = _swiglu_up(cm, w_ffn_gate, w_ffn_up, (layer, 0), tm=1024, tn=FF_TILE)
    z1 = ffn_down_z(a_x, w_down_b, x2, 0, 0, x_seg)
    z1c = ffn_down_z(a_c, w_down_b, ctx2, 0, 0, c_seg)
    _, _, hc = _ln(z1c, *ln0, seg_fn=c_seg, next_mod=mix_mod)

    mu1, rstd1, hx, gg = _ln_proj_gelu(z1, *ln0, x_seg, mix_mod, w_in, (layer,), 2 * mlp_w)
    qk = _proj(hx, w_in, (layer,), 2 * mlp_w, attn_w + kv_w, rope=(cosf, sinf, n))
    vv = _matmul(hx, w_in, (layer,), BF16, tm=1024, tn=512, tk=d, name="in_proj_v",
                 col0=2 * mlp_w + attn_w + kv_w, n=kv_w)
    kvctx = _matmul(hc, w_in, (layer,), BF16, tm=b * c_len, tn=kv_w, tk=d, name="ctx_kv",
                    col0=2 * mlp_w + attn_w)
    ln1 = (ln_gain[layer, 1], ln_bias[layer, 1])
    z2, mu2, rstd2, hm2 = _mixer_out(
        gg, qk, vv, kvctx, w_spatial[layer].astype(BF16), b_spatial[layer].T, sink_logit[layer],
        g_mix[layer].reshape(1, mlp_w + attn_w), w_out_b, z1, (mu1, rstd1) + ln0, mod(1, 2),
        ln1, (mod(2, 0), mod(2, 1)), alpha, b, n, n_groups, n_kv, q_per_kv, c_len)

    a_2, _ = _swiglu_up(hm2, w_ffn_gate, w_ffn_up, (layer, 1), tm=UP_TM, tn=FF_TILE)
    z3 = ffn_down_z(a_2, w_down_b, z2, 1, 2, x_seg, norm=(mu2, rstd2) + ln1)
    out = _ln(z3, ln_gain[layer, 2], ln_bias[layer, 2])
    return out.reshape(b, n, d)
```

```python
import functools

import jax
import jax.numpy as jnp
import numpy as np
from jax import lax
from jax.experimental import pallas as pl
from jax.experimental.pallas import tpu as pltpu

F32 = jnp.float32
BF16 = jnp.bfloat16

HEAD_DIM = 128
CHUNK = 128
WINDOW = 128
BLOCK = 128
GRID_W = 64
ROPE_BASE = 10000.0
EPS = 1e-6
N_SUB = 3
N_MOD = 3

VMEM_LIMIT_BYTES = 58 * 1024 * 1024
LARGE_VMEM_LIMIT_BYTES = 62 * 1024 * 1024
FF_TILE = 256
UP_PREP_SLAB = 64
UP_TM = 2048
PROJ_ROW_CHUNKS = 4
ATTN_HEAD_GROUP = 2
OUT_PROJ_CHUNKS = 8
LN_ROW_CHUNK = 16


def _cparams(sem, vmem_limit_bytes=VMEM_LIMIT_BYTES):
    return pltpu.CompilerParams(dimension_semantics=sem, vmem_limit_bytes=vmem_limit_bytes)


def _ada_kernel(c_ref, w_ref, b_ref, o_ref):
    c = c_ref[...]
    s = (c * jax.nn.sigmoid(c)).astype(BF16)
    o_ref[...] = jnp.dot(s, w_ref[...].astype(BF16), preferred_element_type=F32) + b_ref[...]


def _ada(c8, w_ada, b_ada, tn=512):
    d, n = w_ada.shape
    return pl.pallas_call(
        _ada_kernel,
        grid=(n // tn,),
        in_specs=[pl.BlockSpec((8, d), lambda j: (0, 0)),
                  pl.BlockSpec((d, tn), lambda j: (0, j)),
                  pl.BlockSpec((1, tn), lambda j: (0, j))],
        out_specs=pl.BlockSpec((8, tn), lambda j: (0, j)),
        out_shape=jax.ShapeDtypeStruct((8, n), F32),
        compiler_params=_cparams(("arbitrary",)),
        name="ada",
    )(c8, w_ada, b_ada.reshape(1, n))


def _modcast_kernel(x_ref, shift_ref, scale_ref, o_ref):
    o_ref[...] = (x_ref[...] * (1.0 + scale_ref[...]) + shift_ref[...]).astype(BF16)


def _modcast(x2d, shift, scale, seg_fn, tm=256):
    r, d = x2d.shape
    vec = pl.BlockSpec((None, 1, d), lambda i: (seg_fn(i * tm), 0, 0))
    return pl.pallas_call(
        _modcast_kernel,
        grid=(r // tm,),
        in_specs=[pl.BlockSpec((tm, d), lambda i: (i, 0)), vec, vec],
        out_specs=pl.BlockSpec((tm, d), lambda i: (i, 0)),
        out_shape=jax.ShapeDtypeStruct((r, d), BF16),
        compiler_params=_cparams(("arbitrary",)),
        name="modcast",
    )(x2d, shift, scale)


def _swiglu_up_kernel(x_ref, wg_ref, wu_ref, *rest):
    o_ref = rest[len(rest) // 2]
    x = x_ref[...]
    g = jnp.dot(x, wg_ref[...].astype(BF16), preferred_element_type=F32)
    u = jnp.dot(x, wu_ref[...].astype(BF16), preferred_element_type=F32)
    o_ref[...] = (g * jax.nn.sigmoid(g) * u).astype(BF16)
    n_side = len(rest) // 2
    for src_ref, dst_ref in zip(rest[:n_side], rest[n_side + 1:]):
        dst_ref[...] = src_ref[...].astype(BF16)


def _swiglu_up(xm, w_gate, w_up, lead, tm, tn, side_casts=()):
    r, d = xm.shape
    f = w_gate.shape[-1]
    ni, nj = r // tm, f // tn
    wspec = pl.BlockSpec((None,) * len(lead) + (d, tn), lambda i, j: lead + (0, j))
    side_specs, side_shapes = [], []
    for w in side_casts:
        rows = -(-w.shape[0] // (ni * nj))
        rows = -(-rows // 16) * 16
        n_slabs = w.shape[0] // rows
        assert n_slabs * rows == w.shape[0] and n_slabs <= ni * nj
        side_specs.append(pl.BlockSpec((rows, w.shape[1]),
                                       lambda i, j, n_slabs=n_slabs: (jnp.minimum(i * nj + j, n_slabs - 1), 0)))
        side_shapes.append(jax.ShapeDtypeStruct(w.shape, BF16))
    outs = pl.pallas_call(
        _swiglu_up_kernel,
        grid=(ni, nj),
        in_specs=[pl.BlockSpec((tm, d), lambda i, j: (i, 0)), wspec, wspec] + side_specs,
        out_specs=[pl.BlockSpec((tm, tn), lambda i, j: (i, j))] + side_specs,
        out_shape=[jax.ShapeDtypeStruct((r, f), BF16)] + side_shapes,
        compiler_params=_cparams(("arbitrary", "arbitrary"), LARGE_VMEM_LIMIT_BYTES),
        name="swiglu_up",
    )(xm, w_gate, w_up, *side_casts)
    return outs[0], outs[1:]


def _modcast_swiglu_up_kernel(x_ref, shift_ref, scale_ref, wg_ref, wu_ref, *rest, slab):
    n_side = (len(rest) - 3) // 2
    side_in, o_ref, side_out = rest[:n_side], rest[n_side], rest[n_side + 1:2 * n_side + 1]
    xm_even, xm_odd = rest[2 * n_side + 1:]
    i = pl.program_id(0)
    j = pl.program_id(1)
    n_slabs = xm_even.shape[0] // slab

    def prepare_slab(xm):
        row0 = pl.multiple_of(jnp.minimum(j, n_slabs - 1) * slab, slab)
        xm[pl.ds(row0, slab), :] = (x_ref[...] * (1.0 + scale_ref[...]) + shift_ref[...]).astype(BF16)

    def project(xm):
        x = xm[...]
        g = jnp.dot(x, wg_ref[...].astype(BF16), preferred_element_type=F32)
        u = jnp.dot(x, wu_ref[...].astype(BF16), preferred_element_type=F32)
        o_ref[...] = (g * jax.nn.sigmoid(g) * u).astype(BF16)

    def side_stages(xm_next):
        prepare_slab(xm_next)
        for src_ref, dst_ref in zip(side_in, side_out):
            dst_ref[...] = src_ref[...].astype(BF16)

    @pl.when(i == 0)
    def _():
        side_stages(xm_even)

    @pl.when((i > 0) & (i % 2 == 0))
    def _():
        project(xm_odd)
        side_stages(xm_even)

    @pl.when(i % 2 == 1)
    def _():
        project(xm_even)
        side_stages(xm_odd)


def _modcast_swiglu_up(x2d, shift, scale, seg_fn, w_gate, w_up, lead, tm, tn, slab, side_casts=()):
    r, d = x2d.shape
    f = w_gate.shape[-1]
    ni, nj = r // tm, f // tn
    n_slabs = tm // slab
    assert n_slabs * slab == tm and n_slabs <= nj
    prep_tile = lambda i: jnp.minimum(i, ni - 1)
    wspec = pl.BlockSpec((None,) * len(lead) + (d, tn), lambda i, j: lead + (0, j))
    vec = pl.BlockSpec((None, 1, d), lambda i, j: (seg_fn(prep_tile(i) * tm), 0, 0))
    side_specs, side_shapes = [], []
    for w in side_casts:
        rows = -(-w.shape[0] // (ni * nj))
        rows = -(-rows // 16) * 16
        n_w_slabs = w.shape[0] // rows
        assert n_w_slabs * rows == w.shape[0] and n_w_slabs <= ni * nj
        side_specs.append(pl.BlockSpec((rows, w.shape[1]),
                                       lambda i, j, n=n_w_slabs: (jnp.minimum(i * nj + j, n - 1), 0)))
        side_shapes.append(jax.ShapeDtypeStruct(w.shape, BF16))
    outs = pl.pallas_call(
        functools.partial(_modcast_swiglu_up_kernel, slab=slab),
        grid=(ni + 1, nj),
        in_specs=[pl.BlockSpec((slab, d), lambda i, j: (prep_tile(i) * n_slabs + jnp.minimum(j, n_slabs - 1), 0)),
                  vec, vec, wspec, wspec] + side_specs,
        out_specs=[pl.BlockSpec((tm, tn), lambda i, j: (jnp.maximum(i - 1, 0), j))] + side_specs,
        out_shape=[jax.ShapeDtypeStruct((r, f), BF16)] + side_shapes,
        scratch_shapes=[pltpu.VMEM((tm, d), BF16), pltpu.VMEM((tm, d), BF16)],
        compiler_params=_cparams(("arbitrary", "arbitrary"), LARGE_VMEM_LIMIT_BYTES),
        name="modcast_swiglu_up",
    )(x2d, shift, scale, w_gate, w_up, *side_casts)
    return outs[0], outs[1:]


def _mm_kernel(a_ref, b_ref, o_ref, *, nk):
    d = jnp.dot(a_ref[...], b_ref[...].astype(BF16), preferred_element_type=F32)
    if nk == 1:
        o_ref[...] = d.astype(o_ref.dtype)
    else:
        k = pl.program_id(2)

        @pl.when(k == 0)
        def _():
            o_ref[...] = d

        @pl.when(k > 0)
        def _():
            o_ref[...] += d


def _matmul(a, b, lead, out_dtype, tm, tn, tk, name, col0=0, n=None):
    r, kk = a.shape
    n = b.shape[-1] - col0 if n is None else n
    nk = kk // tk
    assert nk * tk == kk and col0 % tn == 0 and n % tn == 0
    assert nk == 1 or out_dtype == F32
    return pl.pallas_call(
        functools.partial(_mm_kernel, nk=nk),
        grid=(r // tm, n // tn, nk),
        in_specs=[pl.BlockSpec((tm, tk), lambda i, j, k: (i, k)),
                  pl.BlockSpec((None,) * len(lead) + (tk, tn), lambda i, j, k: lead + (k, j + col0 // tn))],
        out_specs=pl.BlockSpec((tm, tn), lambda i, j, k: (i, j)),
        out_shape=jax.ShapeDtypeStruct((r, n), out_dtype),
        compiler_params=_cparams(("arbitrary", "arbitrary", "arbitrary")),
        name=name,
    )(a, b)


def _mm_resid_kernel(*refs, nk, alpha, coef, normed):
    if normed:
        a_ref, b_ref, res_ref, gate_ref, mu_ref, rstd_ref, gain_ref, bias_ref, o_ref = refs
    else:
        a_ref, b_ref, res_ref, gate_ref, o_ref = refs
    def partial_product():
        return jnp.dot(a_ref[...], b_ref[...].astype(BF16), preferred_element_type=F32)

    def finish(y):
        h = res_ref[...]
        if normed:
            h = (h - mu_ref[...]) * rstd_ref[...] * gain_ref[...] + bias_ref[...]
        o_ref[...] = alpha * h + (coef * gate_ref[...]) * y

    if nk == 1:
        finish(partial_product())
    else:
        k = pl.program_id(2)

        @pl.when(k == 0)
        def _():
            o_ref[...] = partial_product()

        if nk > 2:
            @pl.when((k > 0) & (k < nk - 1))
            def _():
                o_ref[...] += partial_product()

        @pl.when(k == nk - 1)
        def _():
            finish(o_ref[...] + partial_product())


def _matmul_resid(a, b, lead, res, gate, seg_fn, alpha, coef, tm, tn, tk, name, norm=None):
    r, kk = a.shape
    n = b.shape[-1]
    nk = kk // tk
    assert nk * tk == kk and nk >= 1 and n % tn == 0
    tile = pl.BlockSpec((tm, tn), lambda i, j, k: (i, j))
    vec = pl.BlockSpec((None, 1, tn), lambda i, j, k: (seg_fn(i * tm), 0, j))
    in_specs = [pl.BlockSpec((tm, tk), lambda i, j, k: (i, k)),
                pl.BlockSpec((None,) * len(lead) + (tk, tn), lambda i, j, k: lead + (k, j)),
                tile, vec]
    args = [a, b, res, gate]
    if norm is not None:
        mu, rstd, gain, bias = norm
        stat = pl.BlockSpec((tm, 1), lambda i, j, k: (i, 0))
        par = pl.BlockSpec((1, tn), lambda i, j, k: (0, j))
        in_specs += [stat, stat, par, par]
        args += [mu, rstd, gain.reshape(1, n), bias.reshape(1, n)]
    return pl.pallas_call(
        functools.partial(_mm_resid_kernel, nk=nk, alpha=alpha, coef=coef, normed=norm is not None),
        grid=(r // tm, n // tn, nk),
        in_specs=in_specs,
        out_specs=tile,
        out_shape=jax.ShapeDtypeStruct((r, n), F32),
        compiler_params=_cparams(("arbitrary", "arbitrary", "arbitrary")),
        name=name,
    )(*args)


def _ln_kernel(*refs, final):
    if final:
        z_ref, gain_ref, bias_ref, o_ref = refs
    else:
        z_ref, gain_ref, bias_ref, shift_ref, scale_ref, mu_ref, rstd_ref, om_ref = refs
    if final:
        mul, add = gain_ref[...], bias_ref[...]
    else:
        mul = gain_ref[...] * (1.0 + scale_ref[...])
        add = bias_ref[...] * (1.0 + scale_ref[...]) + shift_ref[...]
    for r0 in range(0, z_ref.shape[0], LN_ROW_CHUNK):
        rows = slice(r0, r0 + LN_ROW_CHUNK)
        z = z_ref[rows, :]
        mu = jnp.mean(z, axis=-1, keepdims=True)
        zc = z - mu
        var = jnp.mean(zc * zc, axis=-1, keepdims=True)
        rstd = lax.rsqrt(var + EPS)
        y = zc * rstd * mul + add
        if final:
            o_ref[rows, :] = y
        else:
            mu_ref[rows, :] = mu
            rstd_ref[rows, :] = rstd
            om_ref[rows, :] = y.astype(BF16)


def _ln(z, gain, bias, seg_fn=None, next_mod=None, tm=256):
    r, d = z.shape
    row = pl.BlockSpec((tm, d), lambda i: (i, 0))
    full = pl.BlockSpec((1, d), lambda i: (0, 0))
    final = next_mod is None
    in_specs = [row, full, full]
    args = [z, gain.reshape(1, d), bias.reshape(1, d)]
    if final:
        out_specs = row
        out_shape = jax.ShapeDtypeStruct((r, d), F32)
    else:
        vec = pl.BlockSpec((None, 1, d), lambda i: (seg_fn(i * tm), 0, 0))
        stat = pl.BlockSpec((tm, 1), lambda i: (i, 0))
        in_specs += [vec, vec]
        args += list(next_mod)
        out_specs = [stat, stat, row]
        out_shape = [jax.ShapeDtypeStruct((r, 1), F32), jax.ShapeDtypeStruct((r, 1), F32),
                     jax.ShapeDtypeStruct((r, d), BF16)]
    return pl.pallas_call(
        functools.partial(_ln_kernel, final=final),
        grid=(r // tm,),
        in_specs=in_specs,
        out_specs=out_specs,
        out_shape=out_shape,
        compiler_params=_cparams(("arbitrary",)),
        name="ln_final" if final else "ln_stats",
    )(*args)


def _proj_gelu_kernel(x_ref, w_ref, o_ref):
    acc = jnp.dot(x_ref[...], w_ref[...].astype(BF16), preferred_element_type=F32)
    o_ref[...] = (0.5 * acc * (1.0 + lax.erf(acc * np.float32(np.sqrt(0.5))))).astype(BF16)


def _ln_proj_gelu_kernel(z_ref, gain_ref, bias_ref, shift_ref, scale_ref, w_ref,
                         mu_ref, rstd_ref, hm_ref, o_ref, hx_even, hx_odd, *, slab):
    i = pl.program_id(0)
    j = pl.program_id(1)

    def normalise_slab(hx, first=0, last=None):
        mul = gain_ref[...] * (1.0 + scale_ref[...])
        add = bias_ref[...] * (1.0 + scale_ref[...]) + shift_ref[...]
        for r0 in range(first, slab if last is None else last, LN_ROW_CHUNK):
            rows = slice(r0, r0 + LN_ROW_CHUNK)
            z = z_ref[rows, :]
            mu = jnp.mean(z, axis=-1, keepdims=True)
            zc = z - mu
            var = jnp.mean(zc * zc, axis=-1, keepdims=True)
            rstd = lax.rsqrt(var + EPS)
            mu_ref[rows, :] = mu
            rstd_ref[rows, :] = rstd
            y = (zc * rstd * mul + add).astype(BF16)
            hm_ref[rows, :] = y
            hx[pl.ds(pl.multiple_of(j * slab, slab) + r0, LN_ROW_CHUNK), :] = y

    def project_and_normalise(hx_prev, hx_next):
        w = w_ref[...].astype(BF16)
        tm = o_ref.shape[0]
        for c in range(PROJ_ROW_CHUNKS):
            rows = slice(c * tm // PROJ_ROW_CHUNKS, (c + 1) * tm // PROJ_ROW_CHUNKS)
            acc = jnp.dot(hx_prev[rows, :], w, preferred_element_type=F32)
            o_ref[rows, :] = (0.5 * acc * (1.0 + lax.erf(acc * np.float32(np.sqrt(0.5))))).astype(BF16)
            normalise_slab(hx_next, c * slab // PROJ_ROW_CHUNKS, (c + 1) * slab // PROJ_ROW_CHUNKS)

    @pl.when(i == 0)
    def _():
        normalise_slab(hx_even)

    @pl.when((i > 0) & (i % 2 == 0))
    def _():
        project_and_normalise(hx_odd, hx_even)

    @pl.when(i % 2 == 1)
    def _():
        project_and_normalise(hx_even, hx_odd)


def _ln_proj_gelu(z, gain, bias, seg_fn, next_mod, w_in, lead, n, tm=1024, tn=512):
    r, d = z.shape
    n_i, n_j = r // tm, n // tn
    slab = tm // n_j
    assert slab * n_j == tm and slab % LN_ROW_CHUNK == 0
    ln_tile = lambda i: jnp.minimum(i, n_i - 1)
    slab_spec = pl.BlockSpec((slab, d), lambda i, j: (ln_tile(i) * n_j + j, 0))
    stat_spec = pl.BlockSpec((slab, 1), lambda i, j: (ln_tile(i) * n_j + j, 0))
    vec = pl.BlockSpec((None, 1, d), lambda i, j: (seg_fn(ln_tile(i) * tm), 0, 0))
    full = pl.BlockSpec((1, d), lambda i, j: (0, 0))
    return pl.pallas_call(
        functools.partial(_ln_proj_gelu_kernel, slab=slab),
        grid=(n_i + 1, n_j),
        in_specs=[slab_spec, full, full, vec, vec,
                  pl.BlockSpec((None,) * len(lead) + (d, tn), lambda i, j: lead + (0, j))],
        out_specs=[stat_spec, stat_spec, slab_spec,
                   pl.BlockSpec((tm, tn), lambda i, j: (jnp.maximum(i - 1, 0), j))],
        out_shape=[jax.ShapeDtypeStruct((r, 1), F32), jax.ShapeDtypeStruct((r, 1), F32),
                   jax.ShapeDtypeStruct((r, d), BF16), jax.ShapeDtypeStruct((r, n), BF16)],
        scratch_shapes=[pltpu.VMEM((tm, d), BF16), pltpu.VMEM((tm, d), BF16)],
        compiler_params=_cparams(("arbitrary", "arbitrary")),
        name="ln_proj_gelu",
    )(z, gain.reshape(1, d), bias.reshape(1, d), next_mod[0], next_mod[1], w_in)


def _proj_rope_kernel(x_ref, w_ref, cos_ref, sin_ref, o_ref, *, heads_per_tile):
    acc = jnp.dot(x_ref[...], w_ref[...].astype(BF16), preferred_element_type=F32)
    cs = cos_ref[...]
    sn = sin_ref[...]
    for hh in range(heads_per_tile):
        t = acc[:, hh * HEAD_DIM:(hh + 1) * HEAD_DIM]
        rot = pltpu.roll(t, HEAD_DIM // 2, 1)
        o_ref[:, hh * HEAD_DIM:(hh + 1) * HEAD_DIM] = (t * cs + rot * sn).astype(BF16)


def _proj(hx, w_in, lead, col0, n, rope=None, tm=1024, tn=512):
    r, d = hx.shape
    assert col0 % tn == 0 and n % tn == 0
    in_specs = [pl.BlockSpec((tm, d), lambda i, j: (i, 0)),
                pl.BlockSpec((None,) * len(lead) + (d, tn), lambda i, j: lead + (0, j + col0 // tn))]
    args = [hx, w_in]
    if rope is None:
        body, name = _proj_gelu_kernel, "in_proj_gelu"
    else:
        cosf, sinf, seq = rope
        tiles_per_seq = seq // tm
        tab = pl.BlockSpec((tm, HEAD_DIM), lambda i, j: (i % tiles_per_seq, 0))
        in_specs += [tab, tab]
        args += [cosf, sinf]
        body, name = functools.partial(_proj_rope_kernel, heads_per_tile=tn // HEAD_DIM), "in_proj_rope"
    return pl.pallas_call(
        body,
        grid=(r // tm, n // tn),
        in_specs=in_specs,
        out_specs=pl.BlockSpec((tm, tn), lambda i, j: (i, j)),
        out_shape=jax.ShapeDtypeStruct((r, n), BF16),
        compiler_params=_cparams(("arbitrary", "arbitrary")),
        name=name,
    )(*args)


def _mixer_out_kernel(sink_ref, gu_ref, gv_ref, q_ref, kp_ref, kc_ref, kn_ref, vp_ref, vc_ref, vn_ref,
                      kctx_ref, vctx_ref, ws_ref, bs_ref, gmix_ref, band_ref, edge_ref,
                      wout_ref, res_ref, mu_in_ref, rstd_in_ref, gain_in_ref, bias_in_ref, gate_ref,
                      gain_ref, bias_ref, shift_ref, scale_ref,
                      z_ref, mu_ref, rstd_ref, hm_ref, ybuf, ymix,
                      *, n_groups, n_kv, q_per_kv, seq, n_blocks, alpha):
    t = pl.program_id(0)
    blk = jnp.minimum(t, n_blocks - 1) % (seq // BLOCK)
    slot = t % 2
    mlp_w = n_groups * HEAD_DIM
    attn_w = n_kv * q_per_kv * HEAD_DIM

    @pl.when(t == 0)
    def _():
        ymix[1] = jnp.zeros(ymix.shape[1:], ymix.dtype)

    prev_mix = ymix[1 - slot]
    chunk_w = wout_ref.shape[1] // OUT_PROJ_CHUNKS

    def stage_b_chunk(c):
        cols = slice(c * chunk_w, (c + 1) * chunk_w)
        mix = jnp.dot(prev_mix, wout_ref[:, cols], preferred_element_type=F32)
        h = ((res_ref[:, cols] - mu_in_ref[...]) * rstd_in_ref[...] * gain_in_ref[:, cols]
             + bias_in_ref[:, cols])
        z_ref[:, cols] = alpha * h + gate_ref[:, cols] * mix

    def stage_b_stats():
        mul = gain_ref[...] * (1.0 + scale_ref[...])
        add = bias_ref[...] * (1.0 + scale_ref[...]) + shift_ref[...]
        for r0 in range(0, BLOCK, LN_ROW_CHUNK):
            rows = slice(r0, r0 + LN_ROW_CHUNK)
            z = z_ref[rows, :]
            mu = jnp.mean(z, axis=-1, keepdims=True)
            zc = z - mu
            var = jnp.mean(zc * zc, axis=-1, keepdims=True)
            rstd = lax.rsqrt(var + EPS)
            mu_ref[rows, :] = mu
            rstd_ref[rows, :] = rstd
            hm_ref[rows, :] = (zc * rstd * mul + add).astype(BF16)

    stage_b_chunk(0)

    gv = gv_ref[...].astype(F32)
    mu = jnp.mean(gv, axis=-1, keepdims=True)
    vc_ = gv - mu
    var = jnp.mean(vc_ * vc_, axis=-1, keepdims=True)
    vstd = (vc_ * lax.rsqrt(var + EPS)).astype(BF16)
    ss_mlp = jnp.zeros((CHUNK, 1), F32)
    for g in range(n_groups):
        sl = slice(g * HEAD_DIM, (g + 1) * HEAD_DIM)
        mixed = jnp.dot(ws_ref[g], vstd[:, sl], preferred_element_type=F32) + bs_ref[:, g:g + 1]
        ym = gu_ref[:, sl].astype(F32) * mixed
        ss_mlp = ss_mlp + jnp.sum(ym * ym, axis=-1, keepdims=True)
        ybuf[:, sl] = ym

    stage_b_chunk(1)

    edge = (jnp.where(blk == 0, edge_ref[0:1, :], 0.0)
            + jnp.where(blk == seq // BLOCK - 1, edge_ref[1:2, :], 0.0))
    bias = band_ref[...] + edge
    log2e = np.float32(np.log2(np.e))
    scale2 = np.float32(HEAD_DIM ** -0.5) * log2e
    head = lambda ref, qh: ref[:, qh * HEAD_DIM:(qh + 1) * HEAD_DIM]
    n_keys = band_ref.shape[1]
    ones = jnp.ones((n_keys, HEAD_DIM), BF16)
    ss_attn = jnp.zeros((BLOCK, 1), F32)
    next_chunk = 2
    for h0 in range(0, n_kv, ATTN_HEAD_GROUP):
        heads = range(h0, h0 + ATTN_HEAD_GROUP)
        q_all = jnp.stack([jnp.concatenate([head(q_ref, h * q_per_kv + g) for g in range(q_per_kv)], axis=0)
                           for h in heads])
        k_all = jnp.stack([jnp.concatenate([head(r, h) for r in (kctx_ref, kp_ref, kc_ref, kn_ref)], axis=0)
                           for h in heads])
        v_all = jnp.stack([jnp.concatenate(
            [jnp.concatenate([head(r, h) for r in (vctx_ref, vp_ref, vc_ref, vn_ref)], axis=0), ones], axis=1)
            for h in heads])
        sink2 = jnp.stack([jnp.concatenate([jnp.full((BLOCK, 1), sink_ref[h * q_per_kv + g] * log2e, F32)
                                            for g in range(q_per_kv)], axis=0) for h in heads])
        s = jnp.einsum('hqd,hkd->hqk', q_all, k_all, preferred_element_type=F32) * scale2 + bias[None]
        m = jnp.maximum(jnp.max(s, axis=-1, keepdims=True), sink2)
        stage_b_chunk(next_chunk)
        p = jnp.exp2(s - m)
        pv = jnp.einsum('hqk,hkd->hqd', p.astype(BF16), v_all, preferred_element_type=F32)
        stage_b_chunk(next_chunk + 1)
        next_chunk += 2
        denom = pv[:, :, HEAD_DIM:HEAD_DIM + 1] + jnp.exp2(sink2 - m)
        o = pv[:, :, :HEAD_DIM] * (1.0 / denom)
        for i, h in enumerate(heads):
            for g in range(q_per_kv):
                qh = h * q_per_kv + g
                og = o[i, g * BLOCK:(g + 1) * BLOCK, :]
                ss_attn = ss_attn + jnp.sum(og * og, axis=-1, keepdims=True)
                ybuf[:, mlp_w + qh * HEAD_DIM:mlp_w + (qh + 1) * HEAD_DIM] = og

    stage_b_chunk(6)

    rs_mlp = lax.rsqrt(ss_mlp * np.float32(1.0 / mlp_w) + EPS)
    rs_attn = lax.rsqrt(ss_attn * np.float32(1.0 / attn_w) + EPS)
    ymix[slot, :, :mlp_w] = (ybuf[:, :mlp_w] * rs_mlp * gmix_ref[:, :mlp_w]).astype(BF16)
    ymix[slot, :, mlp_w:] = (ybuf[:, mlp_w:] * rs_attn * gmix_ref[:, mlp_w:]).astype(BF16)

    stage_b_chunk(7)
    stage_b_stats()


def _band_masks(q_per_kv, c_len):
    n_keys = c_len + 3 * BLOCK
    r = np.arange(q_per_kv * BLOCK)[:, None] % BLOCK
    col = np.arange(n_keys)[None, :]
    k_off = col - (c_len + BLOCK)
    in_band = (col < c_len) | (np.abs(k_off - r) <= WINDOW)
    band = np.where(in_band, 0.0, -np.inf).astype(np.float32)
    is_prev = (col >= c_len) & (col < c_len + BLOCK)
    is_next = col >= c_len + 2 * BLOCK
    edge = np.where(np.concatenate([is_prev, is_next], axis=0), -np.inf, 0.0).astype(np.float32)
    return jnp.asarray(band), jnp.asarray(edge)


def _mixer_out(gg, qk, vv, kvctx, w_s, b_sT, sink, g_mix, w_out, res, norm_in, gate, ln, next_mod, alpha,
               batch, seq, n_groups, n_kv, q_per_kv, c_len):
    nb = seq // BLOCK
    n_blocks = batch * nb
    mlp_w = n_groups * HEAD_DIM
    attn_w = n_kv * q_per_kv * HEAD_DIM
    d = w_out.shape[-1]
    kv_w = n_kv * HEAD_DIM
    n_keys = c_len + 3 * BLOCK
    band, edge = _band_masks(q_per_kv, c_len)
    mu_in, rstd_in, gain_in, bias_in = norm_in
    cur = lambda t: jnp.minimum(t, n_blocks - 1)
    prev = lambda t: cur(t) - jnp.where(cur(t) % nb == 0, 0, 1)
    nxt = lambda t: cur(t) + jnp.where(cur(t) % nb == nb - 1, 0, 1)
    lag = lambda t: jnp.maximum(t - 1, 0)
    wide = lambda c: pl.BlockSpec((BLOCK, mlp_w), lambda t: (cur(t), c))
    kvs = lambda rfn, c: pl.BlockSpec((BLOCK, kv_w), lambda t: (rfn(t), c))
    k_col = attn_w // kv_w
    const = lambda shape: pl.BlockSpec(shape, lambda t: (0,) * len(shape), pipeline_mode=pl.Buffered(1))
    lag_row =pl.BlockSpec((BLOCK, d), lambda t: (lag(t), 0))
    lag_stat = pl.BlockSpec((BLOCK, 1), lambda t: (lag(t), 0))
    lag_vec = pl.BlockSpec((None, 1, d), lambda t: (lag(t) // nb, 0, 0))
    return pl.pallas_call(
        functools.partial(_mixer_out_kernel, n_groups=n_groups, n_kv=n_kv, q_per_kv=q_per_kv, seq=seq,
                          n_blocks=n_blocks, alpha=alpha),
        grid=(n_blocks + 1,),
        in_specs=[pl.BlockSpec(memory_space=pltpu.SMEM),
                  wide(0), wide(1), pl.BlockSpec((BLOCK, attn_w), lambda t: (cur(t), 0)),
                  kvs(prev, k_col), kvs(cur, k_col), kvs(nxt, k_col),
                  kvs(prev, 0), kvs(cur, 0), kvs(nxt, 0),
                  pl.BlockSpec((c_len, kv_w), lambda t: (cur(t) // nb, 0)),
                  pl.BlockSpec((c_len, kv_w), lambda t: (cur(t) // nb, 1)),
                  const((n_groups, CHUNK, CHUNK)), const((CHUNK, n_groups)), const((1, mlp_w + attn_w)),
                  const((q_per_kv * BLOCK, n_keys)), const((2, n_keys)),
                  const((mlp_w + attn_w, d)),
                  lag_row, lag_stat, lag_stat, const((1, d)), const((1, d)), lag_vec,
                  const((1, d)), const((1, d)), lag_vec, lag_vec],
        out_specs=[lag_row, lag_stat, lag_stat, lag_row],
        out_shape=[jax.ShapeDtypeStruct((n_blocks * BLOCK, d), F32),
                   jax.ShapeDtypeStruct((n_blocks * BLOCK, 1), F32),
                   jax.ShapeDtypeStruct((n_blocks * BLOCK, 1), F32),
                   jax.ShapeDtypeStruct((n_blocks * BLOCK, d), BF16)],
        scratch_shapes=[pltpu.VMEM((BLOCK, mlp_w + attn_w), F32),
                        pltpu.VMEM((2, BLOCK, mlp_w + attn_w), BF16)],
        compiler_params=_cparams(("arbitrary",), LARGE_VMEM_LIMIT_BYTES),
        name="mixer_out",
    )(sink, gg, gg, qk, qk, qk, qk, vv, vv, vv, kvctx, kvctx, w_s, b_sT, g_mix, band, edge,
      w_out, res, mu_in, rstd_in, gain_in.reshape(1, d), bias_in.reshape(1, d), gate,
      ln[0].reshape(1, d), ln[1].reshape(1, d), next_mod[0], next_mod[1])


def _rope_tables(n):
    rows = n // GRID_W
    row = jnp.broadcast_to(jnp.arange(rows, dtype=F32)[:, None], (rows, GRID_W)).reshape(n)
    col = jnp.broadcast_to(jnp.arange(GRID_W, dtype=F32)[None, :], (rows, GRID_W)).reshape(n)
    n_freq = HEAD_DIM // 4
    inv_freq = ROPE_BASE ** (-jnp.arange(n_freq, dtype=F32) / n_freq)
    ang = jnp.concatenate([row[:, None] * inv_freq, col[:, None] * inv_freq], axis=-1)
    cs, sn = jnp.cos(ang), jnp.sin(ang)
    return jnp.concatenate([cs, cs], axis=-1), jnp.concatenate([-sn, sn], axis=-1)


def kernel(x, c, ctx, c_ctx, w_ada, b_ada, w_ffn_gate, w_ffn_up, w_ffn_down, w_in, w_spatial, b_spatial,
           sink_logit, g_mix, w_out, ln_gain, ln_bias):
    b, n, d = x.shape
    c_len = ctx.shape[1]
    depth = w_ada.shape[0]
    d_ff = w_ffn_gate.shape[-1]
    n_groups = w_spatial.shape[1]
    mlp_w = n_groups * HEAD_DIM
    n_q = sink_logit.shape[1]
    attn_w = n_q * HEAD_DIM
    kv_w = (w_in.shape[-1] - 2 * mlp_w - attn_w) // 2
    n_kv = kv_w // HEAD_DIM
    q_per_kv = n_q // n_kv
    assert depth == 1 and b + 1 <= 8
    alpha = float((2.0 * depth) ** 0.25)
    ctx_seg = b

    x_seg = lambda r0: r0 // n
    c_seg = lambda r0: ctx_seg

    cosf, sinf = _rope_tables(n)
    x2 = x.reshape(b * n, d)
    ctx2 = ctx.reshape(b * c_len, d)

    layer = 0
    c8 = jnp.concatenate([c, c_ctx[None, :], jnp.zeros((8 - b - 1, d), F32)], axis=0)
    m = _ada(c8, w_ada[layer], b_ada[layer]).reshape(8, N_SUB, N_MOD, d)
    mod = lambda s, k: m[:, s, k, :][:, None, :]

    def ffn_down_z(a, w_down_b, res, idx, s, seg_fn, norm=None):
        return _matmul_resid(a, w_down_b, (layer, idx), res, mod(s, 2), seg_fn, alpha, 0.5,
                             tm=512, tn=512, tk=d_ff, name="ffn_down", norm=norm)

    cm = _modcast(ctx2, mod(0, 0), mod(0, 1), c_seg)
    mix_mod = (mod(1, 0), mod(1, 1))
    ln0 = (ln_gain[layer, 0], ln_bias[layer, 0])
    a_x, (w_down_b, w_out_b) = _modcast_swiglu_up(
        x2, mod(0, 0), mod(0, 1), x_seg, w_ffn_gate, w_ffn_up, (layer, 0), tm=UP_TM, tn=FF_TILE,
        slab=UP_PREP_SLAB, side_casts=(w_ffn_down.reshape(-1, d), w_out[layer]))
    w_down_b = w_down_b.reshape(w_ffn_down.shape)
    a_c, _ = _swiglu_up(cm, w_ffn_gate, w_ffn_up, (layer, 0), tm=1024, tn=FF_TILE)
    z1 = ffn_down_z(a_x, w_down_b, x2, 0, 0, x_seg)
    z1c = ffn_down_z(a_c, w_down_b, ctx2, 0, 0, c_seg)
    _, _, hc = _ln(z1c, *ln0, seg_fn=c_seg, next_mod=mix_mod)

    mu1, rstd1, hx, gg = _ln_proj_gelu(z1, *ln0, x_seg, mix_mod, w_in, (layer,), 2 * mlp_w)
    qk = _proj(hx, w_in, (layer,), 2 * mlp_w, attn_w + kv_w, rope=(cosf, sinf, n))
    vv = _matmul(hx, w_in, (layer,), BF16, tm=1024, tn=512, tk=d, name="in_proj_v",
                 col0=2 * mlp_w + attn_w + kv_w, n=kv_w)
    kvctx = _matmul(hc, w_in, (layer,), BF16, tm=b * c_len, tn=kv_w, tk=d, name="ctx_kv",
                    col0=2 * mlp_w + attn_w)
    ln1 = (ln_gain[layer, 1], ln_bias[layer, 1])
    z2, mu2, rstd2, hm2 = _mixer_out(
        gg, qk, vv, kvctx, w_spatial[layer].astype(BF16), b_spatial[layer].T, sink_logit[layer],
        g_mix[layer].reshape(1, mlp_w + attn_w), w_out_b, z1, (mu1, rstd1) + ln0, mod(1, 2),
        ln1, (mod(2, 0), mod(2, 1)), alpha, b, n, n_groups, n_kv, q_per_kv, c_len)

    a_2, _ = _swiglu_up(hm2, w_ffn_gate, w_ffn_up, (layer, 1), tm=UP_TM, tn=FF_TILE)
    z3 = ffn_down_z(a_2, w_down_b, z2, 1, 2, x_seg, norm=(mu2, rstd2) + ln1)
    out = _ln(z3, ln_gain[layer, 2], ln_bias[layer, 2])
    return out.reshape(b, n, d)
```

```python
import functools

import jax
import jax.numpy as jnp
import numpy as np
from jax import lax
from jax.experimental import pallas as pl
from jax.experimental.pallas import tpu as pltpu

F32 = jnp.float32
BF16 = jnp.bfloat16

HEAD_DIM = 128
CHUNK = 128
WINDOW = 128
BLOCK = 128
GRID_W = 64
ROPE_BASE = 10000.0
EPS = 1e-6
N_SUB = 3
N_MOD = 3

VMEM_LIMIT_BYTES = 58 * 1024 * 1024
LARGE_VMEM_LIMIT_BYTES = 62 * 1024 * 1024
FF_TILE = 256
UP_TM = 2048
PROJ_ROW_CHUNKS = 4
ATTN_HEAD_GROUP = 2
OUT_PROJ_CHUNKS = 8
LN_ROW_CHUNK = 16


def _cparams(sem, vmem_limit_bytes=VMEM_LIMIT_BYTES):
    return pltpu.CompilerParams(dimension_semantics=sem, vmem_limit_bytes=vmem_limit_bytes)


def _ada_kernel(c_ref, w_ref, b_ref, o_ref):
    c = c_ref[...]
    s = (c * jax.nn.sigmoid(c)).astype(BF16)
    o_ref[...] = jnp.dot(s, w_ref[...].astype(BF16), preferred_element_type=F32) + b_ref[...]


def _ada(c8, w_ada, b_ada, tn=1024):
    d, n = w_ada.shape
    return pl.pallas_call(
        _ada_kernel,
        grid=(n // tn,),
        in_specs=[pl.BlockSpec((8, d), lambda j: (0, 0)),
                  pl.BlockSpec((d, tn), lambda j: (0, j)),
                  pl.BlockSpec((1, tn), lambda j: (0, j))],
        out_specs=pl.BlockSpec((8, tn), lambda j: (0, j)),
        out_shape=jax.ShapeDtypeStruct((8, n), F32),
        compiler_params=_cparams(("arbitrary",)),
        name="ada",
    )(c8, w_ada, b_ada.reshape(1, n))


def _modcast_kernel(x_ref, shift_ref, scale_ref, o_ref):
    o_ref[...] = (x_ref[...] * (1.0 + scale_ref[...]) + shift_ref[...]).astype(BF16)


def _modcast(x2d, shift, scale, seg_fn, tm=512):
    r, d = x2d.shape
    vec = pl.BlockSpec((None, 1, d), lambda i: (seg_fn(i * tm), 0, 0))
    return pl.pallas_call(
        _modcast_kernel,
        grid=(r // tm,),
        in_specs=[pl.BlockSpec((tm, d), lambda i: (i, 0)), vec, vec],
        out_specs=pl.BlockSpec((tm, d), lambda i: (i, 0)),
        out_shape=jax.ShapeDtypeStruct((r, d), BF16),
        compiler_params=_cparams(("arbitrary",)),
        name="modcast",
    )(x2d, shift, scale)


def _swiglu_up_kernel(x_ref, wg_ref, wu_ref, *rest):
    o_ref = rest[len(rest) // 2]
    x = x_ref[...]
    g = jnp.dot(x, wg_ref[...].astype(BF16), preferred_element_type=F32)
    u = jnp.dot(x, wu_ref[...].astype(BF16), preferred_element_type=F32)
    o_ref[...] = (g * jax.nn.sigmoid(g) * u).astype(BF16)
    n_side = len(rest) // 2
    for src_ref, dst_ref in zip(rest[:n_side], rest[n_side + 1:]):
        dst_ref[...] = src_ref[...].astype(BF16)


def _swiglu_up(xm, w_gate, w_up, lead, tm, tn, side_casts=()):
    r, d = xm.shape
    f = w_gate.shape[-1]
    ni, nj = r // tm, f // tn
    wspec = pl.BlockSpec((None,) * len(lead) + (d, tn), lambda i, j: lead + (0, j))
    side_specs, side_shapes = [], []
    for w in side_casts:
        rows = -(-w.shape[0] // (ni * nj))
        rows = -(-rows // 16) * 16
        n_slabs = w.shape[0] // rows
        assert n_slabs * rows == w.shape[0] and n_slabs <= ni * nj
        side_specs.append(pl.BlockSpec((rows, w.shape[1]),
                                       lambda i, j, n_slabs=n_slabs: (jnp.minimum(i * nj + j, n_slabs - 1), 0)))
        side_shapes.append(jax.ShapeDtypeStruct(w.shape, BF16))
    outs = pl.pallas_call(
        _swiglu_up_kernel,
        grid=(ni, nj),
        in_specs=[pl.BlockSpec((tm, d), lambda i, j: (i, 0)), wspec, wspec] + side_specs,
        out_specs=[pl.BlockSpec((tm, tn), lambda i, j: (i, j))] + side_specs,
        out_shape=[jax.ShapeDtypeStruct((r, f), BF16)] + side_shapes,
        compiler_params=_cparams(("arbitrary", "arbitrary"), LARGE_VMEM_LIMIT_BYTES),
        name="swiglu_up",
    )(xm, w_gate, w_up, *side_casts)
    return outs[0], outs[1:]


def _mm_kernel(a_ref, b_ref, o_ref):
    o_ref[...] = jnp.dot(a_ref[...], b_ref[...].astype(BF16), preferred_element_type=F32).astype(o_ref.dtype)


def _matmul(a, b, lead, out_dtype, tm, tn, name, col0=0, n=None):
    r, kk = a.shape
    n = b.shape[-1] - col0 if n is None else n
    assert col0 % tn == 0 and n % tn == 0
    return pl.pallas_call(
        _mm_kernel,
        grid=(r // tm, n // tn),
        in_specs=[pl.BlockSpec((tm, kk), lambda i, j: (i, 0)),
                  pl.BlockSpec((None,) * len(lead) + (kk, tn), lambda i, j: lead + (0, j + col0 // tn))],
        out_specs=pl.BlockSpec((tm, tn), lambda i, j: (i, j)),
        out_shape=jax.ShapeDtypeStruct((r, n), out_dtype),
        compiler_params=_cparams(("arbitrary", "arbitrary")),
        name=name,
    )(a, b)


def _mm_resid_kernel(*refs, alpha, coef, normed):
    if normed:
        a_ref, b_ref, res_ref, gate_ref, mu_ref, rstd_ref, gain_ref, bias_ref, o_ref = refs
    else:
        a_ref, b_ref, res_ref, gate_ref, o_ref = refs
    y = jnp.dot(a_ref[...], b_ref[...].astype(BF16), preferred_element_type=F32)
    h = res_ref[...]
    if normed:
        h = (h - mu_ref[...]) * rstd_ref[...] * gain_ref[...] + bias_ref[...]
    o_ref[...] = alpha * h + (coef * gate_ref[...]) * y


def _matmul_resid(a, b, lead, res, gate, seg_fn, alpha, coef, tm, tn, name, norm=None):
    r, kk = a.shape
    n = b.shape[-1]
    assert n % tn == 0
    tile = pl.BlockSpec((tm, tn), lambda i, j: (i, j))
    vec = pl.BlockSpec((None, 1, tn), lambda i, j: (seg_fn(i * tm), 0, j))
    in_specs = [pl.BlockSpec((tm, kk), lambda i, j: (i, 0)),
                pl.BlockSpec((None,) * len(lead) + (kk, tn), lambda i, j: lead + (0, j)),
                tile, vec]
    args = [a, b, res, gate]
    if norm is not None:
        mu, rstd, gain, bias = norm
        stat = pl.BlockSpec((tm, 1), lambda i, j: (i, 0))
        par = pl.BlockSpec((1, tn), lambda i, j: (0, j))
        in_specs += [stat, stat, par, par]
        args += [mu, rstd, gain.reshape(1, n), bias.reshape(1, n)]
    return pl.pallas_call(
        functools.partial(_mm_resid_kernel, alpha=alpha, coef=coef, normed=norm is not None),
        grid=(r // tm, n // tn),
        in_specs=in_specs,
        out_specs=tile,
        out_shape=jax.ShapeDtypeStruct((r, n), F32),
        compiler_params=_cparams(("arbitrary", "arbitrary")),
        name=name,
    )(*args)


def _ln_kernel(*refs, final):
    if final:
        z_ref, gain_ref, bias_ref, o_ref = refs
    else:
        z_ref, gain_ref, bias_ref, shift_ref, scale_ref, mu_ref, rstd_ref, om_ref = refs
    if final:
        mul, add = gain_ref[...], bias_ref[...]
    else:
        mul = gain_ref[...] * (1.0 + scale_ref[...])
        add = bias_ref[...] * (1.0 + scale_ref[...]) + shift_ref[...]
    for r0 in range(0, z_ref.shape[0], LN_ROW_CHUNK):
        rows = slice(r0, r0 + LN_ROW_CHUNK)
        z = z_ref[rows, :]
        mu = jnp.mean(z, axis=-1, keepdims=True)
        zc = z - mu
        var = jnp.mean(zc * zc, axis=-1, keepdims=True)
        rstd = lax.rsqrt(var + EPS)
        y = zc * rstd * mul + add
        if final:
            o_ref[rows, :] = y
        else:
            mu_ref[rows, :] = mu
            rstd_ref[rows, :] = rstd
            om_ref[rows, :] = y.astype(BF16)


def _ln(z, gain, bias, seg_fn=None, next_mod=None, tm=256):
    r, d = z.shape
    row = pl.BlockSpec((tm, d), lambda i: (i, 0))
    full = pl.BlockSpec((1, d), lambda i: (0, 0))
    final = next_mod is None
    in_specs = [row, full, full]
    args = [z, gain.reshape(1, d), bias.reshape(1, d)]
    if final:
        out_specs = row
        out_shape = jax.ShapeDtypeStruct((r, d), F32)
    else:
        vec = pl.BlockSpec((None, 1, d), lambda i: (seg_fn(i * tm), 0, 0))
        stat = pl.BlockSpec((tm, 1), lambda i: (i, 0))
        in_specs += [vec, vec]
        args += list(next_mod)
        out_specs = [stat, stat, row]
        out_shape = [jax.ShapeDtypeStruct((r, 1), F32), jax.ShapeDtypeStruct((r, 1), F32),
                     jax.ShapeDtypeStruct((r, d), BF16)]
    return pl.pallas_call(
        functools.partial(_ln_kernel, final=final),
        grid=(r // tm,),
        in_specs=in_specs,
        out_specs=out_specs,
        out_shape=out_shape,
        compiler_params=_cparams(("arbitrary",)),
        name="ln_final" if final else "ln_stats",
    )(*args)


def _ln_proj_gelu_kernel(z_ref, gain_ref, bias_ref, shift_ref, scale_ref, w_ref,
                         mu_ref, rstd_ref, hm_ref, o_ref, hx_even, hx_odd, *, slab):
    i = pl.program_id(0)
    j = pl.program_id(1)

    def normalise_slab(hx, first=0, last=None):
        mul = gain_ref[...] * (1.0 + scale_ref[...])
        add = bias_ref[...] * (1.0 + scale_ref[...]) + shift_ref[...]
        for r0 in range(first, slab if last is None else last, LN_ROW_CHUNK):
            rows = slice(r0, r0 + LN_ROW_CHUNK)
            z = z_ref[rows, :]
            mu = jnp.mean(z, axis=-1, keepdims=True)
            zc = z - mu
            var = jnp.mean(zc * zc, axis=-1, keepdims=True)
            rstd = lax.rsqrt(var + EPS)
            mu_ref[rows, :] = mu
            rstd_ref[rows, :] = rstd
            y = (zc * rstd * mul + add).astype(BF16)
            hm_ref[rows, :] = y
            hx[pl.ds(pl.multiple_of(j * slab, slab) + r0, LN_ROW_CHUNK), :] = y

    def project_and_normalise(hx_prev, hx_next):
        w = w_ref[...].astype(BF16)
        tm = o_ref.shape[0]
        for c in range(PROJ_ROW_CHUNKS):
            rows = slice(c * tm // PROJ_ROW_CHUNKS, (c + 1) * tm // PROJ_ROW_CHUNKS)
            acc = jnp.dot(hx_prev[rows, :], w, preferred_element_type=F32)
            o_ref[rows, :] = (0.5 * acc * (1.0 + lax.erf(acc * np.float32(np.sqrt(0.5))))).astype(BF16)
            normalise_slab(hx_next, c * slab // PROJ_ROW_CHUNKS, (c + 1) * slab // PROJ_ROW_CHUNKS)

    @pl.when(i == 0)
    def _():
        normalise_slab(hx_even)

    @pl.when((i > 0) & (i % 2 == 0))
    def _():
        project_and_normalise(hx_odd, hx_even)

    @pl.when(i % 2 == 1)
    def _():
        project_and_normalise(hx_even, hx_odd)


def _ln_proj_gelu(z, gain, bias, seg_fn, next_mod, w_in, lead, n, tm=1024, tn=512):
    r, d = z.shape
    n_i, n_j = r // tm, n // tn
    slab = tm // n_j
    assert slab * n_j == tm and slab % LN_ROW_CHUNK == 0
    ln_tile = lambda i: jnp.minimum(i, n_i - 1)
    slab_spec = pl.BlockSpec((slab, d), lambda i, j: (ln_tile(i) * n_j + j, 0))
    stat_spec = pl.BlockSpec((slab, 1), lambda i, j: (ln_tile(i) * n_j + j, 0))
    vec = pl.BlockSpec((None, 1, d), lambda i, j: (seg_fn(ln_tile(i) * tm), 0, 0))
    full = pl.BlockSpec((1, d), lambda i, j: (0, 0))
    return pl.pallas_call(
        functools.partial(_ln_proj_gelu_kernel, slab=slab),
        grid=(n_i + 1, n_j),
        in_specs=[slab_spec, full, full, vec, vec,
                  pl.BlockSpec((None,) * len(lead) + (d, tn), lambda i, j: lead + (0, j))],
        out_specs=[stat_spec, stat_spec, slab_spec,
                   pl.BlockSpec((tm, tn), lambda i, j: (jnp.maximum(i - 1, 0), j))],
        out_shape=[jax.ShapeDtypeStruct((r, 1), F32), jax.ShapeDtypeStruct((r, 1), F32),
                   jax.ShapeDtypeStruct((r, d), BF16), jax.ShapeDtypeStruct((r, n), BF16)],
        scratch_shapes=[pltpu.VMEM((tm, d), BF16), pltpu.VMEM((tm, d), BF16)],
        compiler_params=_cparams(("arbitrary", "arbitrary")),
        name="ln_proj_gelu",
    )(z, gain.reshape(1, d), bias.reshape(1, d), next_mod[0], next_mod[1], w_in)


def _proj_rope_kernel(x_ref, w_ref, cos_ref, sin_ref, o_ref, *, heads_per_tile):
    acc = jnp.dot(x_ref[...], w_ref[...].astype(BF16), preferred_element_type=F32)
    cs = cos_ref[...]
    sn = sin_ref[...]
    for hh in range(heads_per_tile):
        t = acc[:, hh * HEAD_DIM:(hh + 1) * HEAD_DIM]
        rot = pltpu.roll(t, HEAD_DIM // 2, 1)
        o_ref[:, hh * HEAD_DIM:(hh + 1) * HEAD_DIM] = (t * cs + rot * sn).astype(BF16)


def _proj_rope(hx, w_in, lead, col0, n, cosf, sinf, seq, tm=1024, tn=512):
    r, d = hx.shape
    assert col0 % tn == 0 and n % tn == 0
    tiles_per_seq = seq // tm
    tab = pl.BlockSpec((tm, HEAD_DIM), lambda i, j: (i % tiles_per_seq, 0))
    return pl.pallas_call(
        functools.partial(_proj_rope_kernel, heads_per_tile=tn // HEAD_DIM),
        grid=(r // tm, n // tn),
        in_specs=[pl.BlockSpec((tm, d), lambda i, j: (i, 0)),
                  pl.BlockSpec((None,) * len(lead) + (d, tn), lambda i, j: lead + (0, j + col0 // tn)),
                  tab, tab],
        out_specs=pl.BlockSpec((tm, tn), lambda i, j: (i, j)),
        out_shape=jax.ShapeDtypeStruct((r, n), BF16),
        compiler_params=_cparams(("arbitrary", "arbitrary")),
        name="in_proj_rope",
    )(hx, w_in, cosf, sinf)


def _mixer_out_kernel(sink_ref, gu_ref, gv_ref, q_ref, kp_ref, kc_ref, kn_ref, vp_ref, vc_ref, vn_ref,
                      kctx_ref, vctx_ref, ws_ref, bs_ref, gmix_ref, band_ref, edge_ref,
                      wout_ref, res_ref, mu_in_ref, rstd_in_ref, gain_in_ref, bias_in_ref, gate_ref,
                      gain_ref, bias_ref, shift_ref, scale_ref,
                      z_ref, mu_ref, rstd_ref, hm_ref, ybuf, ymix,
                      *, n_groups, n_kv, q_per_kv, seq, n_blocks, alpha):
    t = pl.program_id(0)
    blk = jnp.minimum(t, n_blocks - 1) % (seq // BLOCK)
    slot = t % 2
    mlp_w = n_groups * HEAD_DIM
    attn_w = n_kv * q_per_kv * HEAD_DIM

    @pl.when(t == 0)
    def _():
        ymix[1] = jnp.zeros(ymix.shape[1:], ymix.dtype)

    prev_mix = ymix[1 - slot]
    chunk_w = wout_ref.shape[1] // OUT_PROJ_CHUNKS

    def stage_b_chunk(c):
        cols = slice(c * chunk_w, (c + 1) * chunk_w)
        mix = jnp.dot(prev_mix, wout_ref[:, cols], preferred_element_type=F32)
        h = ((res_ref[:, cols] - mu_in_ref[...]) * rstd_in_ref[...] * gain_in_ref[:, cols]
             + bias_in_ref[:, cols])
        z_ref[:, cols] = alpha * h + gate_ref[:, cols] * mix

    def stage_b_stats():
        mul = gain_ref[...] * (1.0 + scale_ref[...])
        add = bias_ref[...] * (1.0 + scale_ref[...]) + shift_ref[...]
        for r0 in range(0, BLOCK, LN_ROW_CHUNK):
            rows = slice(r0, r0 + LN_ROW_CHUNK)
            z = z_ref[rows, :]
            mu = jnp.mean(z, axis=-1, keepdims=True)
            zc = z - mu
            var = jnp.mean(zc * zc, axis=-1, keepdims=True)
            rstd = lax.rsqrt(var + EPS)
            mu_ref[rows, :] = mu
            rstd_ref[rows, :] = rstd
            hm_ref[rows, :] = (zc * rstd * mul + add).astype(BF16)

    stage_b_chunk(0)

    gv = gv_ref[...].astype(F32)
    mu = jnp.mean(gv, axis=-1, keepdims=True)
    vc_ = gv - mu
    var = jnp.mean(vc_ * vc_, axis=-1, keepdims=True)
    vstd = (vc_ * lax.rsqrt(var + EPS)).astype(BF16)
    ss_mlp = jnp.zeros((CHUNK, 1), F32)
    for g in range(n_groups):
        sl = slice(g * HEAD_DIM, (g + 1) * HEAD_DIM)
        mixed = jnp.dot(ws_ref[g], vstd[:, sl], preferred_element_type=F32) + bs_ref[:, g:g + 1]
        ym = gu_ref[:, sl].astype(F32) * mixed
        ss_mlp = ss_mlp + jnp.sum(ym * ym, axis=-1, keepdims=True)
        ybuf[:, sl] = ym

    stage_b_chunk(1)

    edge = (jnp.where(blk == 0, edge_ref[0:1, :], 0.0)
            + jnp.where(blk == seq // BLOCK - 1, edge_ref[1:2, :], 0.0))
    bias = band_ref[...] + edge
    log2e = np.float32(np.log2(np.e))
    scale2 = np.float32(HEAD_DIM ** -0.5) * log2e
    head = lambda ref, qh: ref[:, qh * HEAD_DIM:(qh + 1) * HEAD_DIM]
    n_keys = band_ref.shape[1]
    ones = jnp.ones((n_keys, HEAD_DIM), BF16)
    ss_attn = jnp.zeros((BLOCK, 1), F32)
    next_chunk = 2
    for h0 in range(0, n_kv, ATTN_HEAD_GROUP):
        heads = range(h0, h0 + ATTN_HEAD_GROUP)
        q_all = jnp.stack([jnp.concatenate([head(q_ref, h * q_per_kv + g) for g in range(q_per_kv)], axis=0)
                           for h in heads])
        k_all = jnp.stack([jnp.concatenate([head(r, h) for r in (kctx_ref, kp_ref, kc_ref, kn_ref)], axis=0)
                           for h in heads])
        v_all = jnp.stack([jnp.concatenate(
            [jnp.concatenate([head(r, h) for r in (vctx_ref, vp_ref, vc_ref, vn_ref)], axis=0), ones], axis=1)
            for h in heads])
        sink2 = jnp.stack([jnp.concatenate([jnp.full((BLOCK, 1), sink_ref[h * q_per_kv + g] * log2e, F32)
                                            for g in range(q_per_kv)], axis=0) for h in heads])
        s = jnp.einsum('hqd,hkd->hqk', q_all, k_all, preferred_element_type=F32) * scale2 + bias[None]
        m = jnp.maximum(jnp.max(s, axis=-1, keepdims=True), sink2)
        stage_b_chunk(next_chunk)
        p = jnp.exp2(s - m)
        pv = jnp.einsum('hqk,hkd->hqd', p.astype(BF16), v_all, preferred_element_type=F32)
        stage_b_chunk(next_chunk + 1)
        next_chunk += 2
        denom = pv[:, :, HEAD_DIM:HEAD_DIM + 1] + jnp.exp2(sink2 - m)
        o = pv[:, :, :HEAD_DIM] * (1.0 / denom)
        for i, h in enumerate(heads):
            for g in range(q_per_kv):
                qh = h * q_per_kv + g
                og = o[i, g * BLOCK:(g + 1) * BLOCK, :]
                ss_attn = ss_attn + jnp.sum(og * og, axis=-1, keepdims=True)
                ybuf[:, mlp_w + qh * HEAD_DIM:mlp_w + (qh + 1) * HEAD_DIM] = og

    stage_b_chunk(6)

    rs_mlp = lax.rsqrt(ss_mlp * np.float32(1.0 / mlp_w) + EPS)
    rs_attn = lax.rsqrt(ss_attn * np.float32(1.0 / attn_w) + EPS)
    ymix[slot, :, :mlp_w] = (ybuf[:, :mlp_w] * rs_mlp * gmix_ref[:, :mlp_w]).astype(BF16)
    ymix[slot, :, mlp_w:] = (ybuf[:, mlp_w:] * rs_attn * gmix_ref[:, mlp_w:]).astype(BF16)

    stage_b_chunk(7)
    stage_b_stats()


def _band_masks(q_per_kv, c_len):
    n_keys = c_len + 3 * BLOCK
    r = np.arange(q_per_kv * BLOCK)[:, None] % BLOCK
    col = np.arange(n_keys)[None, :]
    k_off = col - (c_len + BLOCK)
    in_band = (col < c_len) | (np.abs(k_off - r) <= WINDOW)
    band = np.where(in_band, 0.0, -np.inf).astype(np.float32)
    is_prev = (col >= c_len) & (col < c_len + BLOCK)
    is_next = col >= c_len + 2 * BLOCK
    edge = np.where(np.concatenate([is_prev, is_next], axis=0), -np.inf, 0.0).astype(np.float32)
    return jnp.asarray(band), jnp.asarray(edge)


def _mixer_out(gg, qk, vv, kvctx, w_s, b_sT, sink, g_mix, w_out, res, norm_in, gate, ln, next_mod, alpha,
               batch, seq, n_groups, n_kv, q_per_kv, c_len):
    nb = seq // BLOCK
    n_blocks = batch * nb
    mlp_w = n_groups * HEAD_DIM
    attn_w = n_kv * q_per_kv * HEAD_DIM
    d = w_out.shape[-1]
    kv_w = n_kv * HEAD_DIM
    n_keys = c_len + 3 * BLOCK
    band, edge = _band_masks(q_per_kv, c_len)
    mu_in, rstd_in, gain_in, bias_in = norm_in
    cur = lambda t: jnp.minimum(t, n_blocks - 1)
    prev = lambda t: cur(t) - jnp.where(cur(t) % nb == 0, 0, 1)
    nxt = lambda t: cur(t) + jnp.where(cur(t) % nb == nb - 1, 0, 1)
    lag = lambda t: jnp.maximum(t - 1, 0)
    wide = lambda c: pl.BlockSpec((BLOCK, mlp_w), lambda t: (cur(t), c))
    kvs = lambda rfn, c: pl.BlockSpec((BLOCK, kv_w), lambda t: (rfn(t), c))
    k_col = attn_w // kv_w
    const = lambda shape: pl.BlockSpec(shape, lambda t: (0,) * len(shape), pipeline_mode=pl.Buffered(1))
    lag_row = pl.BlockSpec((BLOCK, d), lambda t: (lag(t), 0))
    lag_stat = pl.BlockSpec((BLOCK, 1), lambda t: (lag(t), 0))
    lag_vec = pl.BlockSpec((None, 1, d), lambda t: (lag(t) // nb, 0, 0))
    return pl.pallas_call(
        functools.partial(_mixer_out_kernel, n_groups=n_groups, n_kv=n_kv, q_per_kv=q_per_kv, seq=seq,
                          n_blocks=n_blocks, alpha=alpha),
        grid=(n_blocks + 1,),
        in_specs=[pl.BlockSpec(memory_space=pltpu.SMEM),
                  wide(0), wide(1), pl.BlockSpec((BLOCK, attn_w), lambda t: (cur(t), 0)),
                  kvs(prev, k_col), kvs(cur, k_col), kvs(nxt, k_col),
                  kvs(prev, 0), kvs(cur, 0), kvs(nxt, 0),
                  pl.BlockSpec((c_len, kv_w), lambda t: (cur(t) // nb, 0)),
                  pl.BlockSpec((c_len, kv_w), lambda t: (cur(t) // nb, 1)),
                  const((n_groups, CHUNK, CHUNK)), const((CHUNK, n_groups)), const((1, mlp_w + attn_w)),
                  const((q_per_kv * BLOCK, n_keys)), const((2, n_keys)),
                  const((mlp_w + attn_w, d)),
                  lag_row, lag_stat, lag_stat, const((1, d)), const((1, d)), lag_vec,
                  const((1, d)), const((1, d)), lag_vec, lag_vec],
        out_specs=[lag_row, lag_stat, lag_stat, lag_row],
        out_shape=[jax.ShapeDtypeStruct((n_blocks * BLOCK, d), F32),
                   jax.ShapeDtypeStruct((n_blocks * BLOCK, 1), F32),
                   jax.ShapeDtypeStruct((n_blocks * BLOCK, 1), F32),
                   jax.ShapeDtypeStruct((n_blocks * BLOCK, d), BF16)],
        scratch_shapes=[pltpu.VMEM((BLOCK, mlp_w + attn_w), F32),
                        pltpu.VMEM((2, BLOCK, mlp_w + attn_w), BF16)],
        compiler_params=_cparams(("arbitrary",), LARGE_VMEM_LIMIT_BYTES),
        name="mixer_out",
    )(sink, gg, gg, qk, qk, qk, qk, vv, vv, vv, kvctx, kvctx, w_s, b_sT, g_mix, band, edge,
      w_out, res, mu_in, rstd_in, gain_in.reshape(1, d), bias_in.reshape(1, d), gate,
      ln[0].reshape(1, d), ln[1].reshape(1, d), next_mod[0], next_mod[1])


def _rope_tables(n):
    rows = n // GRID_W
    row = jnp.broadcast_to(jnp.arange(rows, dtype=F32)[:, None], (rows, GRID_W)).reshape(n)
    col = jnp.broadcast_to(jnp.arange(GRID_W, dtype=F32)[None, :], (rows, GRID_W)).reshape(n)
    n_freq = HEAD_DIM // 4
    inv_freq = ROPE_BASE ** (-jnp.arange(n_freq, dtype=F32) / n_freq)
    ang = jnp.concatenate([row[:, None] * inv_freq, col[:, None] * inv_freq], axis=-1)
    cs, sn = jnp.cos(ang), jnp.sin(ang)
    return jnp.concatenate([cs, cs], axis=-1), jnp.concatenate([-sn, sn], axis=-1)


def kernel(x, c, ctx, c_ctx, w_ada, b_ada, w_ffn_gate, w_ffn_up, w_ffn_down, w_in, w_spatial, b_spatial,
           sink_logit, g_mix, w_out, ln_gain, ln_bias):
    b, n, d = x.shape
    c_len = ctx.shape[1]
    depth = w_ada.shape[0]
    d_ff = w_ffn_gate.shape[-1]
    n_groups = w_spatial.shape[1]
    mlp_w = n_groups * HEAD_DIM
    n_q = sink_logit.shape[1]
    attn_w = n_q * HEAD_DIM
    kv_w = (w_in.shape[-1] - 2 * mlp_w - attn_w) // 2
    n_kv = kv_w // HEAD_DIM
    q_per_kv = n_q // n_kv
    assert depth == 1 and b + 1 <= 8
    alpha = float((2.0 * depth) ** 0.25)
    ctx_seg = b

    x_seg = lambda r0: r0 // n
    c_seg = lambda r0: ctx_seg

    cosf, sinf = _rope_tables(n)
    x2 = x.reshape(b * n, d)
    ctx2 = ctx.reshape(b * c_len, d)

    layer = 0
    c8 = jnp.concatenate([c, c_ctx[None, :], jnp.zeros((8 - b - 1, d), F32)], axis=0)
    m = _ada(c8, w_ada[layer], b_ada[layer]).reshape(8, N_SUB, N_MOD, d)
    mod = lambda s, k: m[:, s, k, :][:, None, :]

    def ffn_down_z(a, w_down_b, res, idx, s, seg_fn, norm=None):
        return _matmul_resid(a, w_down_b, (layer, idx), res, mod(s, 2), seg_fn, alpha, 0.5,
                             tm=512, tn=512, name="ffn_down", norm=norm)

    xm = _modcast(x2, mod(0, 0), mod(0, 1), x_seg)
    cm = _modcast(ctx2, mod(0, 0), mod(0, 1), c_seg)
    mix_mod = (mod(1, 0), mod(1, 1))
    ln0 = (ln_gain[layer, 0], ln_bias[layer, 0])
    a_x, (w_down_b, w_out_b) = _swiglu_up(xm, w_ffn_gate, w_ffn_up, (layer, 0), tm=UP_TM, tn=FF_TILE,
                                          side_casts=(w_ffn_down.reshape(-1, d), w_out[layer]))
    w_down_b = w_down_b.reshape(w_ffn_down.shape)
    a_c, _ = _swiglu_up(cm, w_ffn_gate, w_ffn_up, (layer, 0), tm=1024, tn=FF_TILE)
    z1 = ffn_down_z(a_x, w_down_b, x2, 0, 0, x_seg)
    z1c = ffn_down_z(a_c, w_down_b, ctx2, 0, 0, c_seg)
    _, _, hc = _ln(z1c, *ln0, seg_fn=c_seg, next_mod=mix_mod)

    mu1, rstd1, hx, gg = _ln_proj_gelu(z1, *ln0, x_seg, mix_mod, w_in, (layer,), 2 * mlp_w)
    qk = _proj_rope(hx, w_in, (layer,), 2 * mlp_w, attn_w + kv_w, cosf, sinf, n)
    vv = _matmul(hx, w_in, (layer,), BF16, tm=1024, tn=512, name="in_proj_v",
                 col0=2 * mlp_w + attn_w + kv_w, n=kv_w)
    kvctx = _matmul(hc, w_in, (layer,), BF16, tm=b * c_len, tn=kv_w, name="ctx_kv",
                    col0=2 * mlp_w + attn_w)
    ln1 = (ln_gain[layer, 1], ln_bias[layer, 1])
    z2, mu2, rstd2, hm2 = _mixer_out(
        gg, qk, vv, kvctx, w_spatial[layer].astype(BF16), b_spatial[layer].T, sink_logit[layer],
        g_mix[layer].reshape(1, mlp_w + attn_w), w_out_b, z1, (mu1, rstd1) + ln0, mod(1, 2),
        ln1, (mod(2, 0), mod(2, 1)), alpha, b, n, n_groups, n_kv, q_per_kv, c_len)

    a_2, _ = _swiglu_up(hm2, w_ffn_gate, w_ffn_up, (layer, 1), tm=UP_TM, tn=FF_TILE)
    z3 = ffn_down_z(a_2, w_down_b, z2, 1, 2, x_seg, norm=(mu2, rstd2) + ln1)
    out = _ln(z3, ln_gain[layer, 2], ln_bias[layer, 2], tm=512)
    return out.reshape(b, n, d)
```
